```python
import jax
import jax.numpy as jnp
from jax import lax
import numpy as np

D_MODEL = 4096
BATCH = 4
SEQ = 4096
DEPTH = 4

N_MIXERS = 4
N_A = (DEPTH + 3) // 4
N_B = (DEPTH + 2) // 4
N_C = (DEPTH + 1) // 4
N_D = DEPTH // 4
ALPHA = (2 * DEPTH) ** 0.25
BETA = (8 * DEPTH) ** -0.25
LN_EPS = 1e-5
RMS_EPS = 1e-6

ML_HEADS = 8
ML_QK = D_MODEL // (2 * ML_HEADS)
ML_V = D_MODEL // ML_HEADS
ML_CHUNK = 64
ML_IN = 2 * ML_HEADS * ML_QK + 2 * ML_HEADS * ML_V + 2 * ML_HEADS

RG_WIDTH = D_MODEL
RG_HEADS = 16
RG_BLOCK = RG_WIDTH // RG_HEADS
RG_CONV = 4
RG_C = 8.0

FX_HD = 128
FX_HEADS = D_MODEL // FX_HD
FX_QBLOCK = 128
FX_IN = 3 * D_MODEL + FX_HEADS

HG_HD = 128
HG_HEADS = D_MODEL // HG_HD
HG_CHUNK = 32

N_EXPERTS = 64
EXPERT_FF = (3 * D_MODEL) // N_EXPERTS
SHARED_FF = EXPERT_FF
TOP_K = 8
N_GROUPS = 8
TOPK_GROUPS = 4
ROUTED_SCALE = 2.5

kernel_name = 'hybrid_mlstm_rglru_fox_hgrn2_moe_deepnorm'


def layer_norm(x, g, b):
    xf = x.astype(jnp.float32)
    mu = jnp.mean(xf, -1, keepdims=True)
    var = jnp.mean(jnp.square(xf - mu), -1, keepdims=True)
    return ((xf - mu) * lax.rsqrt(var + LN_EPS) * g + b).astype(x.dtype)


def rms_norm(x, g):
    xf = x.astype(jnp.float32)
    return xf * lax.rsqrt(jnp.mean(xf * xf, -1, keepdims=True) + RMS_EPS) * g


def to_chunks(t, n_heads, head_dim, chunk):
    B, S, _ = t.shape
    return t.astype(jnp.float32).reshape(B, S // chunk, chunk, n_heads, head_dim).transpose(1, 0, 3, 2, 4)


def from_chunks(t):
    nc, B, H, L, d = t.shape
    return t.transpose(1, 0, 3, 2, 4).reshape(B, nc * L, H, d)


def mlstm_mixer(x, w_in, b_gates, norm_g, w_out):
    B, S, _ = x.shape
    H, dk, dv, L = ML_HEADS, ML_QK, ML_V, ML_CHUNK
    q, k, v, og, gates = jnp.split(x @ w_in, [H * dk, 2 * H * dk, 2 * H * dk + H * dv, 2 * H * dk + 2 * H * dv], axis=-1)
    gates = gates.astype(jnp.float32) + b_gates
    log_i = gates[..., :H]
    log_f = jax.nn.log_sigmoid(gates[..., H:])
    qc = to_chunks(q, H, dk, L) * (dk ** -0.5)
    kc = to_chunks(k, H, dk, L)
    vc = to_chunks(v, H, dv, L)
    ic = to_chunks(log_i, H, 1, L)[..., 0]
    fc = to_chunks(log_f, H, 1, L)[..., 0]
    causal = jnp.tril(jnp.ones((L, L), dtype=bool))

    def step(carry, inp):
        C, n, m = carry
        q_, k_, v_, li, lf = inp
        g = jnp.cumsum(lf, axis=-1)
        g_last = g[..., -1]
        d_intra = jnp.where(causal, g[..., :, None] - g[..., None, :] + li[..., None, :], -jnp.inf)
        d_inter = g + m[..., None]
        m_t = jnp.maximum(d_inter, jnp.max(d_intra, -1))
        w_inter = jnp.exp(d_inter - m_t)
        s = jnp.einsum('bhtk,bhsk->bhts', q_, k_) * jnp.exp(d_intra - m_t[..., None])
        num = w_inter[..., None] * jnp.einsum('bhtk,bhkv->bhtv', q_, C) + jnp.einsum('bhts,bhsv->bhtv', s, v_)
        den = w_inter * jnp.einsum('bhtk,bhk->bht', q_, n) + jnp.sum(s, -1)
        h = num / jnp.maximum(jnp.abs(den), jnp.exp(-m_t))[..., None]
        u = g_last[..., None] - g + li
        m_new = jnp.maximum(g_last + m, jnp.max(u, -1))
        w_u = jnp.exp(u - m_new[..., None])
        decay = jnp.exp(g_last + m - m_new)
        C_new = decay[..., None, None] * C + jnp.einsum('bhs,bhsk,bhsv->bhkv', w_u, k_, v_)
        n_new = decay[..., None] * n + jnp.einsum('bhs,bhsk->bhk', w_u, k_)
        return (C_new, n_new, m_new), h

    init = (jnp.zeros((B, H, dk, dv), jnp.float32), jnp.zeros((B, H, dk), jnp.float32), jnp.zeros((B, H), jnp.float32))
    _, hc = lax.scan(step, init, (qc, kc, vc, ic, fc))
    h = rms_norm(from_chunks(hc), norm_g).reshape(B, S, H * dv)
    h = h * jax.nn.sigmoid(og.astype(jnp.float32))
    return h.astype(x.dtype) @ w_out


def rglru_mixer(x, w_in, conv_w, conv_b, w_ga, b_ga, w_gx, b_gx, lam, w_out):
    B, S, _ = x.shape
    y_branch, u = jnp.split(x @ w_in, 2, axis=-1)
    y_branch = jax.nn.gelu(y_branch.astype(jnp.float32))
    u = lax.conv_general_dilated(u, conv_w[:, None, :], window_strides=(1,), padding=[(RG_CONV - 1, 0)],
                                 dimension_numbers=('NWC', 'WIO', 'NWC'), feature_group_count=RG_WIDTH) + conv_b
    ub = u.reshape(B, S, RG_HEADS, RG_BLOCK)
    r = jax.nn.sigmoid((jnp.einsum('bshi,hij->bshj', ub, w_ga) + b_ga).astype(jnp.float32)).reshape(B, S, RG_WIDTH)
    i = jax.nn.sigmoid((jnp.einsum('bshi,hij->bshj', ub, w_gx) + b_gx).astype(jnp.float32)).reshape(B, S, RG_WIDTH)
    log_a = -RG_C * r * jax.nn.softplus(-lam.astype(jnp.float32))
    a = jnp.exp(log_a)
    first = (jnp.arange(S) == 0)[None, :, None]
    mult = jnp.where(first, 1.0, jnp.sqrt(-jnp.expm1(2.0 * log_a)))
    b = mult * i * u.astype(jnp.float32)

    def combine(left, right):
        a1, b1 = left
        a2, b2 = right
        return a1 * a2, a2 * b1 + b2

    _, h = lax.associative_scan(combine, (a, b), axis=1)
    return (h * y_branch).astype(x.dtype) @ w_out


def fox_mixer(x, w_in, b_f, w_out):
    B, S, _ = x.shape
    H, hd, nq = FX_HEADS, FX_HD, FX_QBLOCK
    q, k, v, fg = jnp.split(x @ w_in, [H * hd, 2 * H * hd, 3 * H * hd], axis=-1)
    heads = lambda t: t.reshape(B, S, H, hd).transpose(0, 2, 1, 3)
    q, k, v = heads(q) * (hd ** -0.5), heads(k), heads(v)
    log_f = jax.nn.log_sigmoid(fg.astype(jnp.float32) + b_f)
    F = jnp.cumsum(log_f, axis=1).transpose(0, 2, 1)
    outs = []
    for qb in range(S // nq):
        lo, hi = qb * nq, (qb + 1) * nq
        s = jnp.einsum('bhtd,bhsd->bhts', q[:, :, lo:hi], k[:, :, :hi]).astype(jnp.float32)
        s = s + F[:, :, lo:hi, None] - F[:, :, None, :hi]
        mask = jnp.arange(lo, hi)[:, None] >= jnp.arange(hi)[None, :]
        p = jax.nn.softmax(jnp.where(mask, s, -jnp.inf), axis=-1)
        outs.append(jnp.einsum('bhts,bhsv->bhtv', p.astype(v.dtype), v[:, :, :hi]))
    o = jnp.concatenate(outs, axis=2).transpose(0, 2, 1, 3).reshape(B, S, H * hd)
    return o @ w_out


def hgrn_lower_bounds(gamma):
    p = jax.nn.softmax(gamma.astype(jnp.float32), axis=0)
    return jnp.cumsum(p, axis=0) - p[0]


def hgrn2_mixer(x, w_in, lower_bound, norm_g, w_out):
    B, S, _ = x.shape
    H, d, L = HG_HEADS, HG_HD, HG_CHUNK
    q, f_pre, i, g = jnp.split(x @ w_in, 4, axis=-1)
    q = jax.nn.silu(q.astype(jnp.float32))
    log_f = jnp.logaddexp(jnp.log(lower_bound), jnp.log1p(-lower_bound) + jax.nn.log_sigmoid(f_pre.astype(jnp.float32)))
    k = -jnp.expm1(log_f)
    qc, kc, vc, fc = (to_chunks(t, H, d, L) for t in (q, k, i, log_f))
    causal = jnp.tril(jnp.ones((L, L), dtype=bool))

    def step(S_prev, inp):
        q_, k_, v_, lf = inp
        cum = jnp.cumsum(lf, axis=2)
        inter = jnp.einsum('bhtk,bhkv->bhtv', q_ * jnp.exp(cum), S_prev)
        diff = cum[:, :, :, None, :] - cum[:, :, None, :, :]
        decay = jnp.exp(jnp.where(causal[:, :, None], diff, -jnp.inf))
        A = jnp.einsum('bhtk,bhsk,bhtsk->bhts', q_, k_, decay)
        o = inter + jnp.einsum('bhts,bhsv->bhtv', A, v_)
        tot = cum[:, :, -1]
        S_new = jnp.exp(tot)[..., None] * S_prev + jnp.einsum('bhsk,bhsv->bhkv', k_ * jnp.exp(tot[:, :, None] - cum), v_)
        return S_new, o

    _, oc = lax.scan(step, jnp.zeros((B, H, d, d), jnp.float32), (qc, kc, vc, fc))
    o = rms_norm(from_chunks(oc), norm_g).reshape(B, S, H * d) * jax.nn.silu(g.astype(jnp.float32))
    return o.astype(x.dtype) @ w_out


def moe(x, w_router, router_bias, w_gate, w_up, w_down, ws_gate, ws_up, ws_down):
    B, S, D = x.shape
    t = x.reshape(B * S, D)
    scores = jax.nn.sigmoid((t @ w_router).astype(jnp.float32))
    sel = scores + router_bias.astype(jnp.float32)
    grp = sel.reshape(-1, N_GROUPS, N_EXPERTS // N_GROUPS)
    grp_score = jnp.sum(lax.top_k(grp, 2)[0], -1)
    _, gidx = lax.top_k(grp_score, TOPK_GROUPS)
    gmask = jnp.sum(jax.nn.one_hot(gidx, N_GROUPS, dtype=jnp.float32), -2) > 0
    emask = jnp.repeat(gmask, N_EXPERTS // N_GROUPS, axis=-1)
    _, eidx = lax.top_k(jnp.where(emask, sel, -jnp.inf), TOP_K)
    w = jnp.take_along_axis(scores, eidx, axis=-1)
    w = w / jnp.sum(w, -1, keepdims=True) * ROUTED_SCALE
    combine = jnp.einsum('tk,tke->te', w, jax.nn.one_hot(eidx, N_EXPERTS, dtype=jnp.float32)).astype(x.dtype)
    h = jax.nn.silu(jnp.einsum('td,edf->tef', t, w_gate)) * jnp.einsum('td,edf->tef', t, w_up)
    routed = jnp.einsum('tef,te,efd->td', h, combine, w_down)
    shared = (jax.nn.silu(t @ ws_gate) * (t @ ws_up)) @ ws_down
    return (routed + shared).reshape(B, S, D)


def setup_inputs(seed: int = 0) -> dict:
    key = jax.random.key(seed)
    ks = iter(jax.random.split(key, 40))
    f32 = jnp.float32
    D = D_MODEL

    def nrm(shape, scale):
        return scale * jax.random.normal(next(ks), shape, f32)

    x = nrm((BATCH, SEQ, D), 1.0)
    ml_w_in = nrm((N_A, D, ML_IN), D ** -0.5)
    gate_base = jnp.concatenate([jnp.zeros((ML_HEADS,), f32), jnp.linspace(3.0, 6.0, ML_HEADS, dtype=f32)])
    ml_b_gates = gate_base[None, :] + nrm((N_A, 2 * ML_HEADS), 0.1)
    ml_norm_g = 1.0 + nrm((N_A, ML_HEADS, ML_V), 0.02)
    ml_w_out = nrm((N_A, ML_HEADS * ML_V, D), BETA * (ML_HEADS * ML_V) ** -0.5)
    rg_w_in = nrm((N_B, D, 2 * RG_WIDTH), D ** -0.5)
    rg_conv_w = nrm((N_B, RG_CONV, RG_WIDTH), RG_CONV ** -0.5)
    rg_conv_b = nrm((N_B, RG_WIDTH), 0.02)
    rg_w_ga = nrm((N_B, RG_HEADS, RG_BLOCK, RG_BLOCK), RG_BLOCK ** -0.5)
    rg_b_ga = nrm((N_B, RG_HEADS, RG_BLOCK), 0.02)
    rg_w_gx = nrm((N_B, RG_HEADS, RG_BLOCK, RG_BLOCK), RG_BLOCK ** -0.5)
    rg_b_gx = nrm((N_B, RG_HEADS, RG_BLOCK), 0.02)
    a_c = jax.random.uniform(next(ks), (N_B, RG_WIDTH), f32, 0.9, 0.999)
    a_base = a_c ** (1.0 / RG_C)
    rg_lambda = jnp.log(a_base) - jnp.log1p(-a_base)
    rg_w_out = nrm((N_B, RG_WIDTH, D), BETA * RG_WIDTH ** -0.5)
    fx_w_in = nrm((N_C, D, FX_IN), D ** -0.5)
    fx_b_f = jnp.linspace(1.0, 4.0, FX_HEADS, dtype=f32)[None, :] + nrm((N_C, FX_HEADS), 0.1)
    fx_w_out = nrm((N_C, D, D), BETA * D ** -0.5)
    hg_w_in = nrm((N_D, D, 4 * D), D ** -0.5)
    hg_gamma = nrm((DEPTH, D), 0.02)
    hg_norm_g = 1.0 + nrm((N_D, HG_HEADS, HG_HD), 0.02)
    hg_w_out = nrm((N_D, D, D), BETA * D ** -0.5)
    ln1_g = 1.0 + nrm((DEPTH, D), 0.02)
    ln1_b = nrm((DEPTH, D), 0.02)
    ln2_g = 1.0 + nrm((DEPTH, D), 0.02)
    ln2_b = nrm((DEPTH, D), 0.02)
    moe_w_router = nrm((DEPTH, D, N_EXPERTS), D ** -0.5)
    moe_router_bias = nrm((DEPTH, N_EXPERTS), 0.01)
    moe_w_gate = nrm((DEPTH, N_EXPERTS, D, EXPERT_FF), D ** -0.5)
    moe_w_up = nrm((DEPTH, N_EXPERTS, D, EXPERT_FF), D ** -0.5)
    moe_w_down = nrm((DEPTH, N_EXPERTS, EXPERT_FF, D), BETA * EXPERT_FF ** -0.5)
    moe_ws_gate = nrm((DEPTH, D, SHARED_FF), D ** -0.5)
    moe_ws_up = nrm((DEPTH, D, SHARED_FF), D ** -0.5)
    moe_ws_down = nrm((DEPTH, SHARED_FF, D), BETA * SHARED_FF ** -0.5)
    return {'x': x,
            'ml_w_in': ml_w_in, 'ml_b_gates': ml_b_gates, 'ml_norm_g': ml_norm_g, 'ml_w_out': ml_w_out,
            'rg_w_in': rg_w_in, 'rg_conv_w': rg_conv_w, 'rg_conv_b': rg_conv_b, 'rg_w_ga': rg_w_ga, 'rg_b_ga': rg_b_ga,
            'rg_w_gx': rg_w_gx, 'rg_b_gx': rg_b_gx, 'rg_lambda': rg_lambda, 'rg_w_out': rg_w_out,
            'fx_w_in': fx_w_in, 'fx_b_f': fx_b_f, 'fx_w_out': fx_w_out,
            'hg_w_in': hg_w_in, 'hg_gamma': hg_gamma, 'hg_norm_g': hg_norm_g, 'hg_w_out': hg_w_out,
            'ln1_g': ln1_g, 'ln1_b': ln1_b, 'ln2_g': ln2_g, 'ln2_b': ln2_b,
            'moe_w_router': moe_w_router, 'moe_router_bias': moe_router_bias, 'moe_w_gate': moe_w_gate,
            'moe_w_up': moe_w_up, 'moe_w_down': moe_w_down, 'moe_ws_gate': moe_ws_gate, 'moe_ws_up': moe_ws_up,
            'moe_ws_down': moe_ws_down}


def reference(x, ml_w_in, ml_b_gates, ml_norm_g, ml_w_out,
              rg_w_in, rg_conv_w, rg_conv_b, rg_w_ga, rg_b_ga, rg_w_gx, rg_b_gx, rg_lambda, rg_w_out,
              fx_w_in, fx_b_f, fx_w_out,
              hg_w_in, hg_gamma, hg_norm_g, hg_w_out,
              ln1_g, ln1_b, ln2_g, ln2_b,
              moe_w_router, moe_router_bias, moe_w_gate, moe_w_up, moe_w_down, moe_ws_gate, moe_ws_up, moe_ws_down):
    lbs = hgrn_lower_bounds(hg_gamma)
    h = x
    for layer in range(DEPTH):
        kind, j = layer % N_MIXERS, layer // N_MIXERS
        if kind == 0:
            mix = mlstm_mixer(h, ml_w_in[j], ml_b_gates[j], ml_norm_g[j], ml_w_out[j])
        elif kind == 1:
            mix = rglru_mixer(h, rg_w_in[j], rg_conv_w[j], rg_conv_b[j], rg_w_ga[j], rg_b_ga[j],
                              rg_w_gx[j], rg_b_gx[j], rg_lambda[j], rg_w_out[j])
        elif kind == 2:
            mix = fox_mixer(h, fx_w_in[j], fx_b_f[j], fx_w_out[j])
        else:
            mix = hgrn2_mixer(h, hg_w_in[j], lbs[layer], hg_norm_g[j], hg_w_out[j])
        h = layer_norm(ALPHA * h + mix, ln1_g[layer], ln1_b[layer])
        ffn = moe(h, moe_w_router[layer], moe_router_bias[layer], moe_w_gate[layer], moe_w_up[layer],
                  moe_w_down[layer], moe_ws_gate[layer], moe_ws_up[layer], moe_ws_down[layer])
        h = layer_norm(ALPHA * h + ffn, ln2_g[layer], ln2_b[layer])
    return h
```

```python
import functools

import jax
import jax.numpy as jnp
from jax import lax
from jax.experimental import pallas as pl
from jax.experimental.pallas import tpu as pltpu

F32 = jnp.float32
BF16 = jnp.bfloat16
I32 = jnp.int32
U32 = jnp.uint32

V7X_VMEM_BYTES = 64 * 1024 * 1024
VMEM_LIMIT_BIG = V7X_VMEM_BYTES - 8 * 1024 * 1024
VMEM_LIMIT_MID = 40 * 1024 * 1024

LN_EPS = 1e-5
RMS_EPS = 1e-6
RG_C = 8.0
ROUTED_SCALE = 2.5
N_GROUPS = 8
TOPK_GROUPS = 4
TOP_K = 8

ML_CHUNK = 64
ML_BLOCK = 512
RG_TILE = 256
FX_BLOCK = 512
HG_CHUNK = 256
HG_BASE = 8
MOE_TM = 256
DISPATCH_TT = 1024
COMBINE_TT = 128
NEG_BIG = -1e30


def _cparams(sem, vmem=VMEM_LIMIT_MID):
    return pltpu.CompilerParams(dimension_semantics=sem, vmem_limit_bytes=vmem)


def _sigmoid(x):
    return 1.0 / (1.0 + jnp.exp(-x))


def _silu(x):
    return x * _sigmoid(x)


def _log_sigmoid(x):
    return jnp.minimum(x, 0.0) - jnp.log1p(jnp.exp(-jnp.abs(x)))


def _softplus(x):
    return jnp.maximum(x, 0.0) + jnp.log1p(jnp.exp(-jnp.abs(x)))


def _expm1(x):
    th = jnp.tanh(0.5 * x)
    return 2.0 * th / (1.0 - th)


def _gelu_tanh(x):
    return 0.5 * x * (1.0 + jnp.tanh(0.7978845608028654 * (x + 0.044715 * x * x * x)))


def _dot(a, b):
    return jnp.dot(a, b, preferred_element_type=F32)


def _dot_nt(a, b):
    return lax.dot_general(a, b, (((1,), (1,)), ((), ())), preferred_element_type=F32)


def _dot_tn(a, b):
    return lax.dot_general(a, b, (((0,), (0,)), ((), ())), preferred_element_type=F32)


def _iota(shape, dim):
    return lax.broadcasted_iota(I32, shape, dim)


def _row_to_col(row):
    n = row.shape[1]
    eye = _iota((n, n), 0) == _iota((n, n), 1)
    return jnp.sum(jnp.where(eye, row, 0.0), axis=1, keepdims=True)


def _col_to_row(col):
    n = col.shape[0]
    eye = _iota((n, n), 0) == _iota((n, n), 1)
    return jnp.sum(jnp.where(eye, col, 0.0), axis=0, keepdims=True)


def _shift_rows(x, d, fill):
    rolled = pltpu.roll(x, d, 0)
    return jnp.where(_iota(x.shape, 0) >= d, rolled, fill)


def _pack_bf16_pair(lo, hi):
    lo_b = lax.bitcast_convert_type(lo.astype(BF16).astype(F32), U32) >> 16
    hi_b = lax.bitcast_convert_type(hi.astype(BF16).astype(F32), U32)
    return lo_b | hi_b


def _unpack_bf16_pair(w):
    lo = lax.bitcast_convert_type(w << 16, F32)
    hi = lax.bitcast_convert_type(w & jnp.uint32(0xFFFF0000), F32)
    return lo, hi


def _mm_kernel(x_ref, w_ref, o_ref):
    o_ref[...] = _dot(x_ref[...], w_ref[...]).astype(o_ref.dtype)


def _matmul(x, w, out_dtype, bm=1024, bn=1024):
    m, k = x.shape
    n = w.shape[1]
    bm, bn = min(bm, m), min(bn, n)
    assert m % bm == 0 and n % bn == 0
    return pl.pallas_call(
        _mm_kernel,
        grid=(n // bn, m // bm),
        in_specs=[pl.BlockSpec((bm, k), lambda j, i: (i, 0)),
                  pl.BlockSpec((k, bn), lambda j, i: (0, j))],
        out_specs=pl.BlockSpec((bm, bn), lambda j, i: (i, j)),
        out_shape=jax.ShapeDtypeStruct((m, n), out_dtype),
        compiler_params=_cparams(("parallel", "parallel"), VMEM_LIMIT_BIG),
        name="matmul",
    )(x, w)


def _gates_kernel(w_ref, x_ref, b_ref, o_ref, *, ls_from):
    z = _dot_nt(w_ref[...], x_ref[...]) + b_ref[...]
    row = _iota(z.shape, 0)
    o_ref[...] = jnp.where(row >= ls_from, _log_sigmoid(z), z)


def _gates_t(x, w_t, bias_col, ls_from, bt=1024):
    t, d = x.shape
    g = w_t.shape[0]
    bt = min(bt, t)
    return pl.pallas_call(
        functools.partial(_gates_kernel, ls_from=ls_from),
        grid=(t // bt,),
        in_specs=[pl.BlockSpec((g, d), lambda i: (0, 0)),
                  pl.BlockSpec((bt, d), lambda i: (i, 0)),
                  pl.BlockSpec((g, 1), lambda i: (0, 0))],
        out_specs=pl.BlockSpec((g, bt), lambda i: (0, i)),
        out_shape=jax.ShapeDtypeStruct((g, t), F32),
        compiler_params=_cparams(("parallel",)),
        name="gates_t",
    )(w_t, x, bias_col)


def _ln_math(x, g, b):
    mu = jnp.mean(x, axis=-1, keepdims=True)
    xc = x - mu
    var = jnp.mean(xc * xc, axis=-1, keepdims=True)
    return xc * lax.rsqrt(var + LN_EPS) * g + b


def _ln_kernel(h_ref, m_ref, g_ref, b_ref, of_ref, ob_ref, ou_ref, *, alpha):
    y = _ln_math(alpha * h_ref[...] + m_ref[...].astype(F32), g_ref[...], b_ref[...])
    of_ref[...] = y
    ob_ref[...] = y.astype(BF16)
    half = y.shape[1] // 2
    ou_ref[...] = _pack_bf16_pair(y[:, :half], y[:, half:])


def _residual_ln(h, mix, g, b, alpha, bt=128):
    t, d = h.shape
    bt = min(bt, t)
    row = lambda i: (i, 0)
    return pl.pallas_call(
        functools.partial(_ln_kernel, alpha=alpha),
        grid=(t // bt,),
        in_specs=[pl.BlockSpec((bt, d), row), pl.BlockSpec((bt, d), row),
                  pl.BlockSpec((1, d), lambda i: (0, 0)), pl.BlockSpec((1, d), lambda i: (0, 0))],
        out_specs=[pl.BlockSpec((bt, d), row), pl.BlockSpec((bt, d), row),
                   pl.BlockSpec((bt, d // 2), row)],
        out_shape=[jax.ShapeDtypeStruct((t, d), F32), jax.ShapeDtypeStruct((t, d), BF16),
                   jax.ShapeDtypeStruct((t, d // 2), U32)],
        compiler_params=_cparams(("parallel",)),
        name="residual_ln",
    )(h, mix, g.reshape(1, d), b.reshape(1, d))


def _mlstm_kernel(q_ref, k_ref, v_ref, og_ref, gi_ref, gf_ref, ng_ref, o_ref,
                  c_ref, n_ref, m_ref, *, chunk, scale):
    @pl.when(pl.program_id(2) == 0)
    def _():
        c_ref[...] = jnp.zeros_like(c_ref)
        n_ref[...] = jnp.zeros_like(n_ref)
        m_ref[...] = jnp.zeros_like(m_ref)

    L = chunk
    n_chunks = q_ref.shape[0] // L
    dv = v_ref.shape[1]
    row = _iota((L, L), 0)
    col = _iota((L, L), 1)
    tri = row >= col
    ng = ng_ref[...]

    def body(j, carry):
        r0 = pl.multiple_of(j * L, L)
        q = q_ref[pl.ds(r0, L), :]
        k = k_ref[pl.ds(r0, L), :]
        v = v_ref[pl.ds(r0, L), :]
        li_row = gi_ref[pl.ds(j, 1), :]
        lf_row = gf_ref[pl.ds(j, 1), :]
        lf_col = _row_to_col(lf_row)
        g_col = jnp.sum(jnp.where(tri, lf_row, 0.0), axis=1, keepdims=True)
        g_row = jnp.sum(jnp.where(row <= col, lf_col, 0.0), axis=0, keepdims=True)
        a_row = li_row - g_row
        a_col = _row_to_col(a_row)
        amax_col = jnp.max(jnp.where(tri, a_row, NEG_BIG), axis=1, keepdims=True)
        m_prev = m_ref[...]
        mm_col = jnp.maximum(amax_col, m_prev)
        e = jnp.where(tri, jnp.exp(jnp.minimum(a_row - mm_col, 0.0)), 0.0)
        w_inter = jnp.exp(m_prev - mm_col)
        qs = (q.astype(F32) * scale).astype(BF16)
        s = _dot_nt(qs, k) * e
        cb = c_ref[...].astype(BF16)
        num = w_inter * _dot(qs, cb) + _dot(s.astype(BF16), v)
        qn = jnp.sum(qs.astype(F32) * n_ref[...], axis=1, keepdims=True)
        den = w_inter * qn + jnp.sum(s, axis=1, keepdims=True)
        m_t = g_col + mm_col
        h = num / jnp.maximum(jnp.abs(den), jnp.exp(-m_t))
        hn = h * lax.rsqrt(jnp.mean(h * h, axis=1, keepdims=True) + RMS_EPS) * ng
        og = og_ref[pl.ds(r0, L), :].astype(F32)
        o_ref[pl.ds(r0, L), :] = (hn * _sigmoid(og)).astype(o_ref.dtype)
        a_last = jnp.max(a_row, axis=1, keepdims=True)
        mm_last = jnp.maximum(a_last, m_prev)
        g_last = jnp.sum(lf_row, axis=1, keepdims=True)
        wu_col = jnp.exp(a_col - mm_last)
        decay = jnp.exp(m_prev - mm_last)
        wv = (wu_col * v.astype(F32)).astype(BF16)
        c_ref[...] = decay * c_ref[...] + _dot_tn(k, wv)
        n_ref[...] = decay * n_ref[...] + jnp.sum(wu_col * k.astype(F32), axis=0, keepdims=True)
        m_ref[...] = g_last + mm_last
        return carry

    lax.fori_loop(0, n_chunks, body, 0)


def _mlstm_cell(p, gates_c, norm_g, batch, seq, heads, dk, dv):
    L = ML_CHUNK
    lb = min(ML_BLOCK, seq)
    nb = seq // lb
    cpb = lb // L
    assert (2 * heads * dk) % dv == 0
    v0 = (2 * heads * dk) // dv
    kern = functools.partial(_mlstm_kernel, chunk=L, scale=float(dk) ** -0.5)
    return pl.pallas_call(
        kern,
        grid=(batch, heads, nb),
        in_specs=[
            pl.BlockSpec((None, lb, dk), lambda b, h, c: (b, c, h)),
            pl.BlockSpec((None, lb, dk), lambda b, h, c: (b, c, heads + h)),
            pl.BlockSpec((None, lb, dv), lambda b, h, c: (b, c, v0 + h)),
            pl.BlockSpec((None, lb, dv), lambda b, h, c: (b, c, v0 + heads + h)),
            pl.BlockSpec((None, cpb, L), lambda b, h, c: (h, b * nb + c, 0)),
            pl.BlockSpec((None, cpb, L), lambda b, h, c: (heads + h, b * nb + c, 0)),
            pl.BlockSpec((None, 1, dv), lambda b, h, c: (h, 0, 0)),
        ],
        out_specs=pl.BlockSpec((None, lb, dv), lambda b, h, c: (b, c, h)),
        out_shape=jax.ShapeDtypeStruct((batch, seq, heads * dv), BF16),
        scratch_shapes=[pltpu.VMEM((dk, dv), F32), pltpu.VMEM((1, dk), F32), pltpu.VMEM((1, 1), F32)],
        compiler_params=_cparams(("parallel", "parallel", "arbitrary")),
        name="mlstm_cell",
    )(p, p, p, p, gates_c, gates_c, norm_g.reshape(heads, 1, dv))


def _mlstm_mixer(hb, w_in, b_gates, norm_g, w_out, batch, seq):
    heads = b_gates.shape[0] // 2
    dv = norm_g.shape[1]
    dk = (w_in.shape[1] - 2 * heads - 2 * heads * dv) // (2 * heads)
    n_main = 2 * heads * dk + 2 * heads * dv
    t = batch * seq
    p = _matmul(hb, w_in[:, :n_main].astype(BF16), BF16)
    w_g_t = w_in[:, n_main:].T.astype(BF16)
    gates = _gates_t(hb, w_g_t, b_gates.reshape(-1, 1), ls_from=heads)
    gates_c = gates.reshape(2 * heads, t // ML_CHUNK, ML_CHUNK)
    cell = _mlstm_cell(p.reshape(batch, seq, n_main), gates_c, norm_g, batch, seq, heads, dk, dv)
    return _matmul(cell.reshape(t, heads * dv), w_out.astype(BF16), F32)


def _rglru_kernel(y_ref, u_ref, cw_ref, cb_ref, wa_ref, ba_ref, wx_ref, bx_ref, lam_ref, o_ref,
                  tail_ref, hc_ref):
    c = pl.program_id(2)

    @pl.when(c == 0)
    def _():
        tail_ref[...] = jnp.zeros_like(tail_ref)
        hc_ref[...] = jnp.zeros_like(hc_ref)

    ts = u_ref.shape[0]
    u = u_ref[...].astype(F32)
    cw = cw_ref[...]
    taps = cw.shape[0]
    ext = jnp.concatenate([tail_ref[...], u], axis=0)
    uc = cw[taps - 1:taps, :] * u + cb_ref[...]
    for j in range(taps - 1):
        d = taps - 1 - j
        uc = uc + cw[j:j + 1, :] * ext[8 - d:8 - d + ts, :]
    tail_ref[...] = u[ts - 8:, :]

    ub = uc.astype(BF16)
    r = _sigmoid(_dot(ub, wa_ref[...]) + ba_ref[...])
    i = _sigmoid(_dot(ub, wx_ref[...]) + bx_ref[...])
    log_a = -RG_C * r * _softplus(-lam_ref[...])
    a = jnp.exp(log_a)
    mult = jnp.sqrt(-_expm1(2.0 * log_a))
    first = jnp.logical_and(c == 0, _iota(a.shape, 0) == 0)
    mult = jnp.where(first, 1.0, mult)
    bv = mult * i * uc
    d = 1
    while d < ts:
        a_sh = _shift_rows(a, d, 1.0)
        b_sh = _shift_rows(bv, d, 0.0)
        bv = a * b_sh + bv
        a = a * a_sh
        d *= 2
    h = bv + a * hc_ref[...]
    hc_ref[...] = h[ts - 1:ts, :]
    o_ref[...] = (h * _gelu_tanh(y_ref[...].astype(F32))).astype(o_ref.dtype)


def _rglru_cell(p, conv_w, conv_b, w_ga, b_ga, w_gx, b_gx, lam, batch, seq):
    heads, blk = b_ga.shape
    width = heads * blk
    ts = min(RG_TILE, seq)
    taps = conv_w.shape[0]
    hmap = lambda b, h, c: (h, 0, 0)
    cmap = lambda b, h, c: (0, h)
    return pl.pallas_call(
        _rglru_kernel,
        grid=(batch, heads, seq // ts),
        in_specs=[
            pl.BlockSpec((None, ts, blk), lambda b, h, c: (b, c, h)),
            pl.BlockSpec((None, ts, blk), lambda b, h, c: (b, c, heads + h)),
            pl.BlockSpec((taps, blk), cmap),
            pl.BlockSpec((1, blk), cmap),
            pl.BlockSpec((None, blk, blk), hmap),
            pl.BlockSpec((None, 1, blk), hmap),
            pl.BlockSpec((None, blk, blk), hmap),
            pl.BlockSpec((None, 1, blk), hmap),
            pl.BlockSpec((1, blk), cmap),
        ],
        out_specs=pl.BlockSpec((None, ts, blk), lambda b, h, c: (b, c, h)),
        out_shape=jax.ShapeDtypeStruct((batch, seq, width), BF16),
        scratch_shapes=[pltpu.VMEM((8, blk), F32), pltpu.VMEM((1, blk), F32)],
        compiler_params=_cparams(("parallel", "parallel", "arbitrary")),
        name="rglru_cell",
    )(p, p, conv_w, conv_b.reshape(1, width), w_ga.astype(BF16), b_ga.reshape(heads, 1, blk),
      w_gx.astype(BF16), b_gx.reshape(heads, 1, blk), lam.reshape(1, width))


def _rglru_mixer(hb, w_in, conv_w, conv_b, w_ga, b_ga, w_gx, b_gx, lam, w_out, batch, seq):
    t = batch * seq
    width = conv_b.shape[0]
    p = _matmul(hb, w_in.astype(BF16), BF16)
    cell = _rglru_cell(p.reshape(batch, seq, 2 * width), conv_w, conv_b, w_ga, b_ga, w_gx, b_gx,
                       lam, batch, seq)
    return _matmul(cell.reshape(t, width), w_out.astype(BF16), F32)


def _cumsum_lanes_kernel(x_ref, o_ref):
    x = x_ref[...]
    n = x.shape[1]
    lane = _iota(x.shape, 1) % 128
    d = 1
    while d < 128:
        x = x + jnp.where(lane >= d, pltpu.roll(x, d, 1), 0.0)
        d *= 2
    carry = jnp.zeros((x.shape[0], 1), F32)
    for b in range(n // 128):
        blk = x[:, b * 128:(b + 1) * 128] + carry
        o_ref[:, b * 128:(b + 1) * 128] = blk
        carry = blk[:, 127:128]


def _cumsum_lanes(x, seg):
    g, t = x.shape
    return pl.pallas_call(
        _cumsum_lanes_kernel,
        grid=(t // seg,),
        in_specs=[pl.BlockSpec((g, seg), lambda i: (0, i))],
        out_specs=pl.BlockSpec((g, seg), lambda i: (0, i)),
        out_shape=jax.ShapeDtypeStruct((g, t), F32),
        compiler_params=_cparams(("parallel",)),
        name="cumsum_lanes",
    )(x)


def _fox_kernel(q_ref, k_ref, v_ref, f_ref, o_ref, *, scale):
    qi = pl.program_id(2)
    tq, hd = q_ref.shape
    tk = tq
    qs = (q_ref[...].astype(F32) * scale).astype(BF16)
    row = _iota((tq, tk), 0)
    col = _iota((tq, tk), 1)

    def body(j, carry):
        m, l, acc = carry
        c0 = pl.multiple_of(j * tk, tk)
        kb = k_ref[pl.ds(c0, tk), :]
        vb = v_ref[pl.ds(c0, tk), :]
        fk = f_ref[:, pl.ds(c0, tk)]
        s = _dot_nt(qs, kb) - fk
        s = jnp.where((qi - j) * tq + row >= col, s, NEG_BIG)
        m_new = jnp.maximum(m, jnp.max(s, axis=1, keepdims=True))
        p = jnp.exp(s - m_new)
        alpha = jnp.exp(m - m_new)
        l = alpha * l + jnp.sum(p, axis=1, keepdims=True)
        acc = alpha * acc + _dot(p.astype(BF16), vb)
        return m_new, l, acc

    init = (jnp.full((tq, 1), NEG_BIG, F32), jnp.zeros((tq, 1), F32), jnp.zeros((tq, hd), F32))
    _, l, acc = lax.fori_loop(0, qi + 1, body, init)
    o_ref[...] = (acc / l).astype(o_ref.dtype)


def _fox_attention(p, f_cum, batch, seq, heads, hd):
    tq = min(FX_BLOCK, seq)
    return pl.pallas_call(
        functools.partial(_fox_kernel, scale=float(hd) ** -0.5),
        grid=(batch, heads, seq // tq),
        in_specs=[
            pl.BlockSpec((None, tq, hd), lambda b, h, i: (b, i, h)),
            pl.BlockSpec((None, seq, hd), lambda b, h, i: (b, 0, heads + h)),
            pl.BlockSpec((None, seq, hd), lambda b, h, i: (b, 0, 2 * heads + h)),
            pl.BlockSpec((None, 1, seq), lambda b, h, i: (h, 0, b)),
        ],
        out_specs=pl.BlockSpec((None, tq, hd), lambda b, h, i: (b, i, h)),
        out_shape=jax.ShapeDtypeStruct((batch, seq, heads * hd), BF16),
        compiler_params=_cparams(("parallel", "parallel", "arbitrary")),
        name="fox_attention",
    )(p, p, p, f_cum)


def _fox_mixer(hb, w_in, b_f, w_out, batch, seq):
    heads = b_f.shape[0]
    d_model = w_out.shape[0]
    hd = d_model // heads
    t = batch * seq
    p = _matmul(hb, w_in[:, :3 * d_model].astype(BF16), BF16)
    w_f_t = w_in[:, 3 * d_model:].T.astype(BF16)
    log_f = _gates_t(hb, w_f_t, b_f.reshape(-1, 1), ls_from=0)
    f_cum = _cumsum_lanes(log_f, seq).reshape(heads, 1, t)
    o = _fox_attention(p.reshape(batch, seq, 3 * d_model), f_cum, batch, seq, heads, hd)
    return _matmul(o.reshape(t, d_model), w_out.astype(BF16), F32)


def _lower_bounds_kernel(g_ref, o_ref):
    g = g_ref[...]
    e = jnp.exp(g - jnp.max(g, axis=0, keepdims=True))
    p = e / jnp.sum(e, axis=0, keepdims=True)
    layer = _iota(p.shape, 0)
    rows = [jnp.sum(jnp.where(layer <= i, p, 0.0), axis=0, keepdims=True) for i in range(g.shape[0])]
    o_ref[...] = jnp.concatenate(rows, axis=0) - p[0:1, :]


def _lower_bounds(gamma):
    return pl.pallas_call(
        _lower_bounds_kernel,
        out_shape=jax.ShapeDtypeStruct(gamma.shape, F32),
        name="hgrn_lower_bounds",
    )(gamma)


def _hgrn_kernel(q_ref, f_ref, i_ref, g_ref, lb_ref, ng_ref, o_ref, s_ref):
    @pl.when(pl.program_id(2) == 0)
    def _():
        s_ref[...] = jnp.zeros_like(s_ref)

    lc, d = q_ref.shape
    q = _silu(q_ref[...].astype(F32))
    lb = lb_ref[...]
    la = jnp.log(lb)
    b_ = jnp.log1p(-lb) + _log_sigmoid(f_ref[...].astype(F32))
    log_f = jnp.maximum(la, b_) + jnp.log1p(jnp.exp(-jnp.abs(la - b_)))
    k = -_expm1(log_f)
    vb = i_ref[...]
    v = vb.astype(F32)
    c = log_f
    sh = 1
    while sh < lc:
        c = c + _shift_rows(c, sh, 0.0)
        sh *= 2
    tot = c[lc - 1:lc, :]

    s_prev = s_ref[...]
    o = _dot((q * jnp.exp(c)).astype(BF16), s_prev.astype(BF16))
    row = _iota((lc, lc), 0)
    col = _iota((lc, lc), 1)
    x = jnp.where(row > col, row ^ col, 0)
    amat = jnp.zeros((lc, lc), F32)
    b = HG_BASE
    while b < lc:
        rho = jnp.broadcast_to(c.reshape(lc // (2 * b), 2 * b, d)[:, b - 1:b, :],
                               (lc // (2 * b), 2 * b, d)).reshape(lc, d)
        qq = (q * jnp.exp(jnp.minimum(c - rho, 0.0))).astype(BF16)
        kk = (k * jnp.exp(jnp.minimum(rho - c, 0.0))).astype(BF16)
        amat = jnp.where(jnp.logical_and(x >= b, x < 2 * b), _dot_nt(qq, kk), amat)
        b *= 2
    o = o + _dot(amat.astype(BF16), vb)
    nb = lc // HG_BASE
    q3 = q.reshape(nb, HG_BASE, d)
    k3 = k.reshape(nb, HG_BASE, d)
    c3 = c.reshape(nb, HG_BASE, d)
    v3 = v.reshape(nb, HG_BASE, d)
    tpos = _iota((nb, HG_BASE, d), 1)
    od = jnp.zeros((nb, HG_BASE, d), F32)
    for s in range(HG_BASE):
        term = q3 * k3[:, s:s + 1, :] * jnp.exp(jnp.minimum(c3 - c3[:, s:s + 1, :], 0.0))
        a_s = jnp.sum(jnp.where(tpos >= s, term, 0.0), axis=2, keepdims=True)
        od = od + a_s * v3[:, s:s + 1, :]
    o = o + od.reshape(lc, d)
    kd = (k * jnp.exp(tot - c)).astype(BF16)
    s_ref[...] = _row_to_col(jnp.exp(tot)) * s_prev + _dot_tn(kd, vb)
    on = o * lax.rsqrt(jnp.mean(o * o, axis=1, keepdims=True) + RMS_EPS) * ng_ref[...]
    o_ref[...] = (on * _silu(g_ref[...].astype(F32))).astype(o_ref.dtype)


def _hgrn_cell(p, lb, norm_g, batch, seq):
    heads, d = norm_g.shape
    lc = min(HG_CHUNK, seq)
    return pl.pallas_call(
        _hgrn_kernel,
        grid=(batch, heads, seq // lc),
        in_specs=[
            pl.BlockSpec((None, lc, d), lambda b, h, c: (b, c, h)),
            pl.BlockSpec((None, lc, d), lambda b, h, c: (b, c, heads + h)),
            pl.BlockSpec((None, lc, d), lambda b, h, c: (b, c, 2 * heads + h)),
            pl.BlockSpec((None, lc, d), lambda b, h, c: (b, c, 3 * heads + h)),
            pl.BlockSpec((1, d), lambda b, h, c: (0, h)),
            pl.BlockSpec((None, 1, d), lambda b, h, c: (h, 0, 0)),
        ],
        out_specs=pl.BlockSpec((None, lc, d), lambda b, h, c: (b, c, h)),
        out_shape=jax.ShapeDtypeStruct((batch, seq, heads * d), BF16),
        scratch_shapes=[pltpu.VMEM((d, d), F32)],
        compiler_params=_cparams(("parallel", "parallel", "arbitrary")),
        name="hgrn_cell",
    )(p, p, p, p, lb.reshape(1, heads * d), norm_g.reshape(heads, 1, d))


def _hgrn_mixer(hb, w_in, lb, norm_g, w_out, batch, seq):
    t = batch * seq
    d_model = w_out.shape[0]
    p = _matmul(hb, w_in.astype(BF16), BF16)
    cell = _hgrn_cell(p.reshape(batch, seq, 4 * d_model), lb, norm_g, batch, seq)
    return _matmul(cell.reshape(t, d_model), w_out.astype(BF16), F32)


def _router_kernel(w_ref, x_ref, b_ref, idx_ref, wt_ref):
    logits = _dot_nt(w_ref[...], x_ref[...])
    n_e, bt = logits.shape
    gs = n_e // N_GROUPS
    scores = _sigmoid(logits)
    sel = scores + b_ref[...]
    sel3 = sel.reshape(N_GROUPS, gs, bt)
    pos3 = _iota(sel3.shape, 1)
    m1 = jnp.max(sel3, axis=1, keepdims=True)
    i1 = jnp.min(jnp.where(sel3 == m1, pos3, gs), axis=1, keepdims=True)
    m2 = jnp.max(jnp.where(pos3 == i1, -jnp.inf, sel3), axis=1, keepdims=True)
    gscore = (m1 + m2).reshape(N_GROUPS, bt)
    gid = _iota(gscore.shape, 0)
    gmask = jnp.zeros(gscore.shape, F32)
    for _ in range(TOPK_GROUPS):
        gm = jnp.max(gscore, axis=0, keepdims=True)
        gi = jnp.min(jnp.where(gscore == gm, gid, N_GROUPS), axis=0, keepdims=True)
        hit = gid == gi
        gmask = jnp.where(hit, 1.0, gmask)
        gscore = jnp.where(hit, -jnp.inf, gscore)
    emask = jnp.broadcast_to(gmask.reshape(N_GROUPS, 1, bt), (N_GROUPS, gs, bt)).reshape(n_e, bt)
    cand = jnp.where(emask > 0.5, sel, -jnp.inf)
    eid = _iota(cand.shape, 0)
    idx_rows, w_rows = [], []
    for _ in range(TOP_K):
        cm = jnp.max(cand, axis=0, keepdims=True)
        ci = jnp.min(jnp.where(cand == cm, eid, n_e), axis=0, keepdims=True)
        hit = eid == ci
        idx_rows.append(ci)
        w_rows.append(jnp.sum(jnp.where(hit, scores, 0.0), axis=0, keepdims=True))
        cand = jnp.where(hit, -jnp.inf, cand)
    w = jnp.concatenate(w_rows, axis=0)
    w = w / jnp.sum(w, axis=0, keepdims=True) * ROUTED_SCALE
    idx_ref[...] = jnp.concatenate(idx_rows, axis=0)
    eye = (_iota((bt, bt), 0) == _iota((bt, bt), 1)).astype(BF16)
    w1 = w.astype(BF16)
    r1 = w - w1.astype(F32)
    w2 = r1.astype(BF16)
    w3 = (r1 - w2.astype(F32)).astype(BF16)
    wt_ref[...] = _dot_nt(eye, w1) + _dot_nt(eye, w2) + _dot_nt(eye, w3)


def _router(hb, w_router_t, bias_col, bt=512):
    t, d = hb.shape
    n_e = w_router_t.shape[0]
    bt = min(bt, t)
    return pl.pallas_call(
        _router_kernel,
        grid=(t // bt,),
        in_specs=[pl.BlockSpec((n_e, d), lambda i: (0, 0)),
                  pl.BlockSpec((bt, d), lambda i: (i, 0)),
                  pl.BlockSpec((n_e, 1), lambda i: (0, 0))],
        out_specs=[pl.BlockSpec((TOP_K, bt), lambda i: (0, i)),
                   pl.BlockSpec((bt, TOP_K), lambda i: (i, 0))],
        out_shape=[jax.ShapeDtypeStruct((TOP_K, t), I32), jax.ShapeDtypeStruct((t, TOP_K), F32)],
        compiler_params=_cparams(("parallel",)),
        name="moe_router",
    )(w_router_t, hb, bias_col)


def _plan_kernel(idx_ref, pos_ref, te_ref, tv_ref, *, n_e, tm, bt):
    k_top, t = idx_ref.shape
    n_steps = t // bt
    eid = _iota((n_e, bt), 0)
    upper = (_iota((bt, bt), 0) <= _iota((bt, bt), 1)).astype(BF16)

    def onehots(j):
        idx = idx_ref[:, pl.ds(pl.multiple_of(j * bt, bt), bt)]
        return [eid == idx[k:k + 1, :] for k in range(k_top)]

    def rank_body(j, carry):
        hits = onehots(j)
        m = jnp.zeros((n_e, bt), F32)
        for hmask in hits:
            m = m + hmask.astype(F32)
        cum = _dot(m.astype(BF16), upper)
        rank = carry + cum - m
        rows = [jnp.sum(jnp.where(hmask, rank, 0.0), axis=0, keepdims=True) for hmask in hits]
        pos_ref[:, pl.ds(pl.multiple_of(j * bt, bt), bt)] = jnp.concatenate(rows, axis=0).astype(I32)
        return carry + cum[:, bt - 1:bt]

    counts = lax.fori_loop(0, n_steps, rank_body, jnp.zeros((n_e, 1), F32))
    shift = tm.bit_length() - 1
    padded = (((counts.astype(I32) + (tm - 1)) >> shift) << shift).astype(F32)
    padded_row = _col_to_row(padded)
    below = _iota((n_e, n_e), 1) < _iota((n_e, n_e), 0)
    off = jnp.sum(jnp.where(below, padded_row, 0.0), axis=1, keepdims=True)

    def off_body(j, carry):
        hits = onehots(j)
        rows = [jnp.sum(jnp.where(hmask, off, 0.0), axis=0, keepdims=True) for hmask in hits]
        sl = pl.ds(pl.multiple_of(j * bt, bt), bt)
        pos_ref[:, sl] = pos_ref[:, sl] + jnp.concatenate(rows, axis=0).astype(I32)
        return carry

    lax.fori_loop(0, n_steps, off_body, 0)
    n_tiles = te_ref.shape[1]
    start = (_iota((n_e, n_tiles), 1) * tm).astype(F32)
    inside = jnp.logical_and(start >= off, start < off + padded)
    ecol = _iota((n_e, n_tiles), 0).astype(F32)
    te_ref[...] = jnp.sum(jnp.where(inside, ecol, 0.0), axis=0, keepdims=True).astype(I32)
    tv_ref[...] = jnp.sum(inside.astype(F32), axis=0, keepdims=True).astype(I32)


def _plan(idx, n_e, tm, n_tiles):
    k_top, t = idx.shape
    bt = min(512, t)
    return pl.pallas_call(
        functools.partial(_plan_kernel, n_e=n_e, tm=tm, bt=bt),
        out_shape=[jax.ShapeDtypeStruct((k_top, t), I32),
                   jax.ShapeDtypeStruct((1, n_tiles), I32),
                   jax.ShapeDtypeStruct((1, n_tiles), I32)],
        compiler_params=pltpu.CompilerParams(vmem_limit_bytes=VMEM_LIMIT_MID),
        name="moe_plan",
    )(idx)


def _dispatch_kernel(pos_ref, x_hbm, xs_hbm, sem):
    k_top, bt = pos_ref.shape
    base = pl.program_id(0) * bt

    def row_copy(t, k):
        return pltpu.make_async_copy(x_hbm.at[pl.ds(base + t, 1)],
                                     xs_hbm.at[pl.ds(pos_ref[k, t], 1)], sem)

    def start_body(t, carry):
        for k in range(k_top):
            row_copy(t, k).start()
        return carry

    lax.fori_loop(0, bt, start_body, 0)

    def wait_body(t, carry):
        for k in range(k_top):
            row_copy(t, k).wait()
        return carry

    lax.fori_loop(0, bt, wait_body, 0)


def _dispatch(pos, x_packed, n_rows):
    k_top, t = pos.shape
    bt = min(DISPATCH_TT, t)
    return pl.pallas_call(
        _dispatch_kernel,
        grid=(t // bt,),
        in_specs=[pl.BlockSpec((k_top, bt), lambda i: (0, i), memory_space=pltpu.SMEM),
                  pl.BlockSpec(memory_space=pl.ANY)],
        out_specs=pl.BlockSpec(memory_space=pl.ANY),
        out_shape=jax.ShapeDtypeStruct((n_rows, x_packed.shape[1]), U32),
        scratch_shapes=[pltpu.SemaphoreType.DMA(())],
        compiler_params=pltpu.CompilerParams(dimension_semantics=("arbitrary",)),
        name="moe_dispatch",
    )(pos, x_packed)


def _experts_kernel(te_ref, tv_ref, x_ref, wg_ref, wu_ref, wd_ref, o_ref):
    @pl.when(tv_ref[pl.program_id(0)] > 0)
    def _():
        half = x_ref.shape[1]
        lo, hi = _unpack_bf16_pair(x_ref[...])
        xl = lo.astype(BF16)
        xh = hi.astype(BF16)
        g = _dot(xl, wg_ref[:half, :]) + _dot(xh, wg_ref[half:, :])
        u = _dot(xl, wu_ref[:half, :]) + _dot(xh, wu_ref[half:, :])
        hmid = (_silu(g) * u).astype(BF16)
        y = _dot(hmid, wd_ref[...])
        o_ref[...] = _pack_bf16_pair(y[:, :half], y[:, half:])


def _experts(xs, tile_expert, tile_valid, w_gate, w_up, w_down, tm):
    n_rows, half = xs.shape
    n_e, d, ff = w_gate.shape
    n_tiles = n_rows // tm
    wmap = lambda i, te, tv: (te[i], 0, 0)
    return pl.pallas_call(
        _experts_kernel,
        grid_spec=pltpu.PrefetchScalarGridSpec(
            num_scalar_prefetch=2,
            grid=(n_tiles,),
            in_specs=[pl.BlockSpec((tm, half), lambda i, te, tv: (i, 0)),
                      pl.BlockSpec((None, d, ff), wmap),
                      pl.BlockSpec((None, d, ff), wmap),
                      pl.BlockSpec((None, ff, d), wmap)],
            out_specs=pl.BlockSpec((tm, half), lambda i, te, tv: (i, 0)),
        ),
        out_shape=jax.ShapeDtypeStruct((n_rows, half), U32),
        compiler_params=_cparams(("arbitrary",)),
        name="moe_experts",
    )(tile_expert, tile_valid, xs, w_gate, w_up, w_down)


def _combine_kernel(pos_ref, wt_ref, h_ref, wsg_ref, wsu_ref, wsd_ref, g_ref, b_ref, y_hbm,
                    of_ref, ob_ref, ybuf, sem, *, alpha):
    k_top, bt = pos_ref.shape

    def row_copy(t, k):
        return pltpu.make_async_copy(y_hbm.at[pl.ds(pos_ref[k, t], 1)],
                                     ybuf.at[k, pl.ds(t, 1)], sem)

    def start_body(t, carry):
        for k in range(k_top):
            row_copy(t, k).start()
        return carry

    lax.fori_loop(0, bt, start_body, 0)

    h = h_ref[...]
    xb = h.astype(BF16)
    mid = (_silu(_dot(xb, wsg_ref[...])) * _dot(xb, wsu_ref[...])).astype(BF16)
    shared = _dot(mid, wsd_ref[...])
    half = shared.shape[1] // 2
    acc_lo = shared[:, :half]
    acc_hi = shared[:, half:]

    def wait_body(t, carry):
        for k in range(k_top):
            row_copy(t, k).wait()
        return carry

    lax.fori_loop(0, bt, wait_body, 0)

    wt = wt_ref[...]
    for k in range(k_top):
        lo, hi = _unpack_bf16_pair(ybuf[k])
        wk = wt[:, k:k + 1]
        acc_lo = acc_lo + wk * lo
        acc_hi = acc_hi + wk * hi
    x = alpha * h + jnp.concatenate([acc_lo, acc_hi], axis=1)
    y = _ln_math(x, g_ref[...], b_ref[...])
    of_ref[...] = y
    ob_ref[...] = y.astype(BF16)


def _combine(pos, wt, h, ws_gate, ws_up, ws_down, g, b, y, alpha):
    k_top, t = pos.shape
    d = h.shape[1]
    ff = ws_gate.shape[1]
    bt = min(COMBINE_TT, t)
    row = lambda i: (i, 0)
    const = lambda i: (0, 0)
    return pl.pallas_call(
        functools.partial(_combine_kernel, alpha=alpha),
        grid=(t // bt,),
        in_specs=[pl.BlockSpec((k_top, bt), lambda i: (0, i), memory_space=pltpu.SMEM),
                  pl.BlockSpec((bt, k_top), row),
                  pl.BlockSpec((bt, d), row),
                  pl.BlockSpec((d, ff), const), pl.BlockSpec((d, ff), const),
                  pl.BlockSpec((ff, d), const),
                  pl.BlockSpec((1, d), const), pl.BlockSpec((1, d), const),
                  pl.BlockSpec(memory_space=pl.ANY)],
        out_specs=[pl.BlockSpec((bt, d), row), pl.BlockSpec((bt, d), row)],
        out_shape=[jax.ShapeDtypeStruct((t, d), F32), jax.ShapeDtypeStruct((t, d), BF16)],
        scratch_shapes=[pltpu.VMEM((k_top, bt, d // 2), U32), pltpu.SemaphoreType.DMA(())],
        compiler_params=_cparams(("arbitrary",)),
        name="moe_combine",
    )(pos, wt, h, ws_gate, ws_up, ws_down, g.reshape(1, d), b.reshape(1, d), y)


def _moe_layer(h, hb, hu, w_router, router_bias, w_gate, w_up, w_down, ws_gate, ws_up, ws_down,
               ln_g, ln_b, alpha):
    t, d = h.shape
    n_e = w_router.shape[1]
    tm = MOE_TM
    n_rows = t * TOP_K + n_e * tm
    n_tiles = n_rows // tm
    idx, wt = _router(hb, w_router.T.astype(BF16), router_bias.reshape(n_e, 1))
    pos, tile_expert, tile_valid = _plan(idx, n_e, tm, n_tiles)
    xs = _dispatch(pos, hu, n_rows)
    y = _experts(xs, tile_expert.reshape(n_tiles), tile_valid.reshape(n_tiles),
                 w_gate.astype(BF16), w_up.astype(BF16), w_down.astype(BF16), tm)
    return _combine(pos, wt, h, ws_gate.astype(BF16), ws_up.astype(BF16), ws_down.astype(BF16),
                    ln_g, ln_b, y, alpha)


def kernel(x, ml_w_in, ml_b_gates, ml_norm_g, ml_w_out, rg_w_in, rg_conv_w, rg_conv_b, rg_w_ga, rg_b_ga, rg_w_gx, rg_b_gx, rg_lambda, rg_w_out, fx_w_in, fx_b_f, fx_w_out, hg_w_in, hg_gamma, hg_norm_g, hg_w_out, ln1_g, ln1_b, ln2_g, ln2_b, moe_w_router, moe_router_bias, moe_w_gate, moe_w_up, moe_w_down, moe_ws_gate, moe_ws_up, moe_ws_down):
    batch, seq, d = x.shape
    depth = ln1_g.shape[0]
    alpha = float((2 * depth) ** 0.25)
    t = batch * seq
    lbs = _lower_bounds(hg_gamma)
    h = x.reshape(t, d)
    hb = h.astype(BF16)
    for layer in range(depth):
        kind, j = layer % 4, layer // 4
        if kind == 0:
            mix = _mlstm_mixer(hb, ml_w_in[j], ml_b_gates[j], ml_norm_g[j], ml_w_out[j], batch, seq)
        elif kind == 1:
            mix = _rglru_mixer(hb, rg_w_in[j], rg_conv_w[j], rg_conv_b[j], rg_w_ga[j], rg_b_ga[j],
                               rg_w_gx[j], rg_b_gx[j], rg_lambda[j], rg_w_out[j], batch, seq)
        elif kind == 2:
            mix = _fox_mixer(hb, fx_w_in[j], fx_b_f[j], fx_w_out[j], batch, seq)
        else:
            mix = _hgrn_mixer(hb, hg_w_in[j], lbs[layer], hg_norm_g[j], hg_w_out[j], batch, seq)
        h, hb, hu = _residual_ln(h, mix, ln1_g[layer], ln1_b[layer], alpha)
        h, hb = _moe_layer(h, hb, hu, moe_w_router[layer], moe_router_bias[layer], moe_w_gate[layer],
                           moe_w_up[layer], moe_w_down[layer], moe_ws_gate[layer], moe_ws_up[layer],
                           moe_ws_down[layer], ln2_g[layer], ln2_b[layer], alpha)
    return h.reshape(batch, seq, d)
```

```python
import functools

import jax
import jax.numpy as jnp
from jax import lax
from jax.experimental import pallas as pl
from jax.experimental.pallas import tpu as pltpu

F32 = jnp.float32
BF16 = jnp.bfloat16
I32 = jnp.int32
U32 = jnp.uint32

V7X_VMEM_BYTES = 64 * 1024 * 1024
VMEM_LIMIT_BIG = V7X_VMEM_BYTES - 8 * 1024 * 1024
VMEM_LIMIT_MID = 40 * 1024 * 1024

LN_EPS = 1e-5
RMS_EPS = 1e-6
RG_C = 8.0
ROUTED_SCALE = 2.5
N_GROUPS = 8
TOPK_GROUPS = 4
TOP_K = 8

ML_CHUNK = 64
ML_BLOCK = 512
RG_TILE = 256
FX_TQ = 512
FX_TK = 512
FX_HEADS_PER_STEP = 4
HG_CHUNK = 256
HG_BASE = 8
MOE_TM = 256
DISPATCH_TT = 512
COMBINE_TT = 128
COMBINE_CHUNK = 256
DMA_UNROLL = 4
NEG_BIG = -1e30
LOG2_E = 1.4426950408889634


def _cparams(sem, vmem=VMEM_LIMIT_MID):
    return pltpu.CompilerParams(dimension_semantics=sem, vmem_limit_bytes=vmem)


def _sigmoid(x):
    return 0.5 * jnp.tanh(0.5 * x) + 0.5


def _silu(x):
    return x * _sigmoid(x)


def _log_sigmoid(x):
    return jnp.minimum(x, 0.0) - jnp.log1p(jnp.exp(-jnp.abs(x)))


def _softplus(x):
    return jnp.maximum(x, 0.0) + jnp.log1p(jnp.exp(-jnp.abs(x)))


def _expm1(x):
    th = jnp.tanh(0.5 * x)
    return 2.0 * th / (1.0 - th)


def _gelu_tanh(x):
    return 0.5 * x * (1.0 + jnp.tanh(0.7978845608028654 * (x + 0.044715 * x * x * x)))


def _dot(a, b):
    return jnp.dot(a, b, preferred_element_type=F32)


def _dot_nt(a, b):
    return lax.dot_general(a, b, (((1,), (1,)), ((), ())), preferred_element_type=F32)


def _dot_tn(a, b):
    return lax.dot_general(a, b, (((0,), (0,)), ((), ())), preferred_element_type=F32)


def _iota(shape, dim):
    return lax.broadcasted_iota(I32, shape, dim)


def _row_to_col(row):
    n = row.shape[1]
    eye = _iota((n, n), 0) == _iota((n, n), 1)
    return jnp.sum(jnp.where(eye, row, 0.0), axis=1, keepdims=True)


def _col_to_row(col):
    n = col.shape[0]
    eye = _iota((n, n), 0) == _iota((n, n), 1)
    return jnp.sum(jnp.where(eye, col, 0.0), axis=0, keepdims=True)


def _shift_rows(x, d, fill):
    rolled = pltpu.roll(x, d, 0)
    return jnp.where(_iota(x.shape, 0) >= d, rolled, fill)


def _pack_bf16_pair(lo, hi):
    lo_b = lax.bitcast_convert_type(lo.astype(BF16).astype(F32), U32) >> 16
    hi_b = lax.bitcast_convert_type(hi.astype(BF16).astype(F32), U32)
    return lo_b | hi_b


def _unpack_bf16_pair(w):
    lo = lax.bitcast_convert_type(w << 16, F32)
    hi = lax.bitcast_convert_type(w & jnp.uint32(0xFFFF0000), F32)
    return lo, hi


def _mm_kernel(x_ref, w_ref, o_ref):
    o_ref[...] = _dot(x_ref[...], w_ref[...]).astype(o_ref.dtype)


def _matmul(x, w, out_dtype, n=None, bm=1024, bn=1024):
    m, k = x.shape
    n = w.shape[1] if n is None else n
    bm, bn = min(bm, m), min(bn, n)
    assert m % bm == 0 and n % bn == 0
    return pl.pallas_call(
        _mm_kernel,
        grid=(n // bn, m // bm),
        in_specs=[pl.BlockSpec((bm, k), lambda j, i: (i, 0)),
                  pl.BlockSpec((k, bn), lambda j, i: (0, j))],
        out_specs=pl.BlockSpec((bm, bn), lambda j, i: (i, j)),
        out_shape=jax.ShapeDtypeStruct((m, n), out_dtype),
        compiler_params=_cparams(("parallel", "parallel"), VMEM_LIMIT_BIG),
        name="matmul",
    )(x, w)


def _gates_kernel(w_ref, x_ref, b_ref, o_ref, *, ls_from):
    z = _dot_nt(w_ref[...], x_ref[...]) + b_ref[...]
    row = _iota(z.shape, 0)
    o_ref[...] = jnp.where(row >= ls_from, _log_sigmoid(z), z)


def _gates_t(x, w_t, bias_col, ls_from, bt=1024):
    t, d = x.shape
    g = w_t.shape[0]
    bt = min(bt, t)
    return pl.pallas_call(
        functools.partial(_gates_kernel, ls_from=ls_from),
        grid=(t // bt,),
        in_specs=[pl.BlockSpec((g, d), lambda i: (0, 0)),
                  pl.BlockSpec((bt, d), lambda i: (i, 0)),
                  pl.BlockSpec((g, 1), lambda i: (0, 0))],
        out_specs=pl.BlockSpec((g, bt), lambda i: (0, i)),
        out_shape=jax.ShapeDtypeStruct((g, t), F32),
        compiler_params=_cparams(("parallel",)),
        name="gates_t",
    )(w_t, x, bias_col)


def _ln_math(x, g, b):
    mu = jnp.mean(x, axis=-1, keepdims=True)
    xc = x - mu
    var = jnp.mean(xc * xc, axis=-1, keepdims=True)
    return xc * lax.rsqrt(var + LN_EPS) * g + b


def _ln_kernel(h_ref, m_ref, g_ref, b_ref, of_ref, ob_ref, ou_ref, *, alpha):
    y = _ln_math(alpha * h_ref[...] + m_ref[...].astype(F32), g_ref[...], b_ref[...])
    of_ref[...] = y
    ob_ref[...] = y.astype(BF16)
    half = y.shape[1] // 2
    ou_ref[...] = _pack_bf16_pair(y[:, :half], y[:, half:])


def _residual_ln(h, mix, g, b, alpha, bt=128):
    t, d = h.shape
    bt = min(bt, t)
    row = lambda i: (i, 0)
    return pl.pallas_call(
        functools.partial(_ln_kernel, alpha=alpha),
        grid=(t // bt,),
        in_specs=[pl.BlockSpec((bt, d), row), pl.BlockSpec((bt, d), row),
                  pl.BlockSpec((1, d), lambda i: (0, 0)), pl.BlockSpec((1, d), lambda i: (0, 0))],
        out_specs=[pl.BlockSpec((bt, d), row), pl.BlockSpec((bt, d), row),
                   pl.BlockSpec((bt, d // 2), row)],
        out_shape=[jax.ShapeDtypeStruct((t, d), F32), jax.ShapeDtypeStruct((t, d), BF16),
                   jax.ShapeDtypeStruct((t, d // 2), U32)],
        compiler_params=_cparams(("parallel",)),
        name="residual_ln",
    )(h, mix, g.reshape(1, d), b.reshape(1, d))


def _mlstm_kernel(q_ref, k_ref, v_ref, og_ref, gi_ref, gf_ref, ng_ref, o_ref,
                  c_ref, n_ref, m_ref, *, chunk, scale):
    @pl.when(pl.program_id(2) == 0)
    def _():
        c_ref[...] = jnp.zeros_like(c_ref)
        n_ref[...] = jnp.zeros_like(n_ref)
        m_ref[...] = jnp.zeros_like(m_ref)

    L = chunk
    n_chunks = q_ref.shape[0] // L
    dv = v_ref.shape[1]
    row = _iota((L, L), 0)
    col = _iota((L, L), 1)
    tri = row >= col
    ng = ng_ref[...]

    def body(j, carry):
        r0 = pl.multiple_of(j * L, L)
        q = q_ref[pl.ds(r0, L), :]
        k = k_ref[pl.ds(r0, L), :]
        v = v_ref[pl.ds(r0, L), :]
        li_row = gi_ref[pl.ds(j, 1), :]
        lf_row = gf_ref[pl.ds(j, 1), :]
        lf_col = _row_to_col(lf_row)
        g_col = jnp.sum(jnp.where(tri, lf_row, 0.0), axis=1, keepdims=True)
        g_row = jnp.sum(jnp.where(row <= col, lf_col, 0.0), axis=0, keepdims=True)
        a_row = li_row - g_row
        a_col = _row_to_col(a_row)
        amax_col = jnp.max(jnp.where(tri, a_row, NEG_BIG), axis=1, keepdims=True)
        m_prev = m_ref[...]
        mm_col = jnp.maximum(amax_col, m_prev)
        e = jnp.where(tri, jnp.exp(jnp.minimum(a_row - mm_col, 0.0)), 0.0)
        w_inter = jnp.exp(m_prev - mm_col)
        qs = (q.astype(F32) * scale).astype(BF16)
        s = _dot_nt(qs, k) * e
        cb = c_ref[...].astype(BF16)
        num = w_inter * _dot(qs, cb) + _dot(s.astype(BF16), v)
        qn = jnp.sum(qs.astype(F32) * n_ref[...], axis=1, keepdims=True)
        den = w_inter * qn + jnp.sum(s, axis=1, keepdims=True)
        m_t = g_col + mm_col
        h = num / jnp.maximum(jnp.abs(den), jnp.exp(-m_t))
        hn = h * lax.rsqrt(jnp.mean(h * h, axis=1, keepdims=True) + RMS_EPS) * ng
        og = og_ref[pl.ds(r0, L), :].astype(F32)
        o_ref[pl.ds(r0, L), :] = (hn * _sigmoid(og)).astype(o_ref.dtype)
        a_last = jnp.max(a_row, axis=1, keepdims=True)
        mm_last = jnp.maximum(a_last, m_prev)
        g_last = jnp.sum(lf_row, axis=1, keepdims=True)
        wu_col = jnp.exp(a_col - mm_last)
        decay = jnp.exp(m_prev - mm_last)
        wv = (wu_col * v.astype(F32)).astype(BF16)
        c_ref[...] = decay * c_ref[...] + _dot_tn(k, wv)
        n_ref[...] = decay * n_ref[...] + jnp.sum(wu_col * k.astype(F32), axis=0, keepdims=True)
        m_ref[...] = g_last + mm_last
        return carry

    lax.fori_loop(0, n_chunks, body, 0)


def _mlstm_cell(p, gates_c, norm_g, batch, seq, heads, dk, dv):
    L = ML_CHUNK
    lb = min(ML_BLOCK, seq)
    nb = seq // lb
    cpb = lb // L
    assert (2 * heads * dk) % dv == 0
    v0 = (2 * heads * dk) // dv
    kern = functools.partial(_mlstm_kernel, chunk=L, scale=float(dk) ** -0.5)
    return pl.pallas_call(
        kern,
        grid=(batch, heads, nb),
        in_specs=[
            pl.BlockSpec((None, lb, dk), lambda b, h, c: (b, c, h)),
            pl.BlockSpec((None, lb, dk), lambda b, h, c: (b, c, heads + h)),
            pl.BlockSpec((None, lb, dv), lambda b, h, c: (b, c, v0 + h)),
            pl.BlockSpec((None, lb, dv), lambda b, h, c: (b, c, v0 + heads + h)),
            pl.BlockSpec((None, cpb, L), lambda b, h, c: (h, b * nb + c, 0)),
            pl.BlockSpec((None, cpb, L), lambda b, h, c: (heads + h, b * nb + c, 0)),
            pl.BlockSpec((None, 1, dv), lambda b, h, c: (h, 0, 0)),
        ],
        out_specs=pl.BlockSpec((None, lb, dv), lambda b, h, c: (b, c, h)),
        out_shape=jax.ShapeDtypeStruct((batch, seq, heads * dv), BF16),
        scratch_shapes=[pltpu.VMEM((dk, dv), F32), pltpu.VMEM((1, dk), F32), pltpu.VMEM((1, 1), F32)],
        compiler_params=_cparams(("parallel", "parallel", "arbitrary")),
        name="mlstm_cell",
    )(p, p, p, p, gates_c, gates_c, norm_g.reshape(heads, 1, dv))


def _mlstm_mixer(hb, w_in, b_gates, norm_g, w_out, batch, seq):
    heads = b_gates.shape[0] // 2
    dv = norm_g.shape[1]
    dk = (w_in.shape[1] - 2 * heads - 2 * heads * dv) // (2 * heads)
    n_main = 2 * heads * dk + 2 * heads * dv
    t = batch * seq
    p = _matmul(hb, w_in.astype(BF16), BF16, n=n_main)
    w_g_t = w_in[:, n_main:].T.astype(BF16)
    gates = _gates_t(hb, w_g_t, b_gates.reshape(-1, 1), ls_from=heads)
    gates_c = gates.reshape(2 * heads, t // ML_CHUNK, ML_CHUNK)
    cell = _mlstm_cell(p.reshape(batch, seq, n_main), gates_c, norm_g, batch, seq, heads, dk, dv)
    return _matmul(cell.reshape(t, heads * dv), w_out.astype(BF16), F32)


def _rglru_kernel(y_ref, u_ref, cw_ref, cb_ref, wa_ref, ba_ref, wx_ref, bx_ref, lam_ref, o_ref,
                  tail_ref, hc_ref):
    c = pl.program_id(2)

    @pl.when(c == 0)
    def _():
        tail_ref[...] = jnp.zeros_like(tail_ref)
        hc_ref[...] = jnp.zeros_like(hc_ref)

    ts = u_ref.shape[0]
    u = u_ref[...].astype(F32)
    cw = cw_ref[...]
    taps = cw.shape[0]
    ext = jnp.concatenate([tail_ref[...], u], axis=0)
    uc = cw[taps - 1:taps, :] * u + cb_ref[...]
    for j in range(taps - 1):
        d = taps - 1 - j
        uc = uc + cw[j:j + 1, :] * ext[8 - d:8 - d + ts, :]
    tail_ref[...] = u[ts - 8:, :]

    ub = uc.astype(BF16)
    r = _sigmoid(_dot(ub, wa_ref[...]) + ba_ref[...])
    i = _sigmoid(_dot(ub, wx_ref[...]) + bx_ref[...])
    log_a = -RG_C * r * _softplus(-lam_ref[...])
    a = jnp.exp(log_a)
    mult = jnp.sqrt(-_expm1(2.0 * log_a))
    first = jnp.logical_and(c == 0, _iota(a.shape, 0) == 0)
    mult = jnp.where(first, 1.0, mult)
    bv = mult * i * uc
    d = 1
    while d < ts:
        a_sh = _shift_rows(a, d, 1.0)
        b_sh = _shift_rows(bv, d, 0.0)
        bv = a * b_sh + bv
        a = a * a_sh
        d *= 2
    h = bv + a * hc_ref[...]
    hc_ref[...] = h[ts - 1:ts, :]
    o_ref[...] = (h * _gelu_tanh(y_ref[...].astype(F32))).astype(o_ref.dtype)


def _rglru_cell(p, conv_w, conv_b, w_ga, b_ga, w_gx, b_gx, lam, batch, seq):
    heads, blk = b_ga.shape
    width = heads * blk
    ts = min(RG_TILE, seq)
    taps = conv_w.shape[0]
    hmap = lambda b, h, c: (h, 0, 0)
    cmap = lambda b, h, c: (0, h)
    return pl.pallas_call(
        _rglru_kernel,
        grid=(batch, heads, seq // ts),
        in_specs=[
            pl.BlockSpec((None, ts, blk), lambda b, h, c: (b, c, h)),
            pl.BlockSpec((None, ts, blk), lambda b, h, c: (b, c, heads + h)),
            pl.BlockSpec((taps, blk), cmap),
            pl.BlockSpec((1, blk), cmap),
            pl.BlockSpec((None, blk, blk), hmap),
            pl.BlockSpec((None, 1, blk), hmap),
            pl.BlockSpec((None, blk, blk), hmap),
            pl.BlockSpec((None, 1, blk), hmap),
            pl.BlockSpec((1, blk), cmap),
        ],
        out_specs=pl.BlockSpec((None, ts, blk), lambda b, h, c: (b, c, h)),
        out_shape=jax.ShapeDtypeStruct((batch, seq, width), BF16),
        scratch_shapes=[pltpu.VMEM((8, blk), F32), pltpu.VMEM((1, blk), F32)],
        compiler_params=_cparams(("parallel", "parallel", "arbitrary")),
        name="rglru_cell",
    )(p, p, conv_w, conv_b.reshape(1, width), w_ga.astype(BF16), b_ga.reshape(heads, 1, blk),
      w_gx.astype(BF16), b_gx.reshape(heads, 1, blk), lam.reshape(1, width))


def _rglru_mixer(hb, w_in, conv_w, conv_b, w_ga, b_ga, w_gx, b_gx, lam, w_out, batch, seq):
    t = batch * seq
    width = conv_b.shape[0]
    p = _matmul(hb, w_in.astype(BF16), BF16)
    cell = _rglru_cell(p.reshape(batch, seq, 2 * width), conv_w, conv_b, w_ga, b_ga, w_gx, b_gx,
                       lam, batch, seq)
    return _matmul(cell.reshape(t, width), w_out.astype(BF16), F32)


def _cumsum_lanes_kernel(x_ref, o_ref):
    x = x_ref[...]
    n = x.shape[1]
    lane = _iota(x.shape, 1) % 128
    d = 1
    while d < 128:
        x = x + jnp.where(lane >= d, pltpu.roll(x, d, 1), 0.0)
        d *= 2
    carry = jnp.zeros((x.shape[0], 1), F32)
    for b in range(n // 128):
        blk = x[:, b * 128:(b + 1) * 128] + carry
        o_ref[:, b * 128:(b + 1) * 128] = blk
        carry = blk[:, 127:128]


def _cumsum_lanes(x, seg):
    g, t = x.shape
    return pl.pallas_call(
        _cumsum_lanes_kernel,
        grid=(t // seg,),
        in_specs=[pl.BlockSpec((g, seg), lambda i: (0, i))],
        out_specs=pl.BlockSpec((g, seg), lambda i: (0, i)),
        out_shape=jax.ShapeDtypeStruct((g, t), F32),
        compiler_params=_cparams(("parallel",)),
        name="cumsum_lanes",
    )(x)


def _fox_kernel(q_ref, k_ref, v_ref, f_ref, o_ref, v1_ref, *, scale, hd, tk):
    qi = pl.program_id(2)
    tq = q_ref.shape[0]
    n_diag = tq // tk
    nh = q_ref.shape[1] // hd

    @pl.when(qi == 0)
    def _():
        for a in range(nh):
            v1_ref[a, :, :hd] = v_ref[:, a * hd:(a + 1) * hd]
            v1_ref[a, :, hd:] = jnp.ones((v_ref.shape[0], hd), v1_ref.dtype)

    qs = [(q_ref[:, a * hd:(a + 1) * hd].astype(F32) * scale).astype(BF16) for a in range(nh)]

    def step(j, carry, diag_off):
        c0 = pl.multiple_of(j * tk, tk)
        out = []
        for a in range(nh):
            m, acc = carry[a]
            kb = k_ref[pl.ds(c0, tk), a * hd:(a + 1) * hd]
            vb = v1_ref[a, pl.ds(c0, tk), :]
            fk = f_ref[a, :, pl.ds(c0, tk)]
            s = _dot_nt(qs[a], kb) - fk
            if diag_off is not None:
                s = jnp.where(_iota((tq, tk), 0) >= _iota((tq, tk), 1) + diag_off, s, NEG_BIG)
            m_new = jnp.maximum(m, jnp.max(s, axis=1, keepdims=True))
            p = jnp.exp((s - m_new).astype(BF16))
            out.append((m_new, jnp.exp(m - m_new) * acc + _dot(p, vb)))
        return tuple(out)

    init = tuple((jnp.full((tq, 1), NEG_BIG, F32), jnp.zeros((tq, 2 * hd), F32)) for _ in range(nh))
    final = lax.fori_loop(0, qi * n_diag, functools.partial(step, diag_off=None), init)
    for r in range(n_diag):
        final = step(qi * n_diag + r, final, r * tk)
    for a in range(nh):
        acc = final[a][1]
        o_ref[:, a * hd:(a + 1) * hd] = (acc[:, :hd] / acc[:, hd:hd + 1]).astype(o_ref.dtype)


def _fox_attention(p, f_cum, batch, seq, heads, hd):
    tq = min(FX_TQ, seq)
    tk = min(FX_TK, tq)
    assert tq % tk == 0
    nh = FX_HEADS_PER_STEP
    assert heads % nh == 0
    hp = heads // nh
    w = nh * hd
    return pl.pallas_call(
        functools.partial(_fox_kernel, scale=float(hd) ** -0.5, hd=hd, tk=tk),
        grid=(batch, hp, seq // tq),
        in_specs=[
            pl.BlockSpec((None, tq, w), lambda b, h, i: (b, i, h)),
            pl.BlockSpec((None, seq, w), lambda b, h, i: (b, 0, hp + h)),
            pl.BlockSpec((None, seq, w), lambda b, h, i: (b, 0, 2 * hp + h)),
            pl.BlockSpec((nh, 1, seq), lambda b, h, i: (h, 0, b)),
        ],
        out_specs=pl.BlockSpec((None, tq, w), lambda b, h, i: (b, i, h)),
        out_shape=jax.ShapeDtypeStruct((batch, seq, heads * hd), BF16),
        scratch_shapes=[pltpu.VMEM((nh, seq, 2 * hd), BF16)],
        compiler_params=_cparams(("parallel", "parallel", "arbitrary"), VMEM_LIMIT_BIG),
        name="fox_attention",
    )(p, p, p, f_cum)


def _fox_mixer(hb, w_in, b_f, w_out, batch, seq):
    heads = b_f.shape[0]
    d_model = w_out.shape[0]
    hd = d_model // heads
    t = batch * seq
    p = _matmul(hb, w_in.astype(BF16), BF16, n=3 * d_model)
    w_f_t = w_in[:, 3 * d_model:].T.astype(BF16)
    log_f = _gates_t(hb, w_f_t, b_f.reshape(-1, 1), ls_from=0)
    f_cum = _cumsum_lanes(log_f, seq).reshape(heads, 1, t)
    o = _fox_attention(p.reshape(batch, seq, 3 * d_model), f_cum, batch, seq, heads, hd)
    return _matmul(o.reshape(t, d_model), w_out.astype(BF16), F32)


def _lower_bounds_kernel(g_ref, o_ref):
    g = g_ref[...]
    e = jnp.exp(g - jnp.max(g, axis=0, keepdims=True))
    p = e / jnp.sum(e, axis=0, keepdims=True)
    layer = _iota(p.shape, 0)
    rows = [jnp.sum(jnp.where(layer <= i, p, 0.0), axis=0, keepdims=True) for i in range(g.shape[0])]
    o_ref[...] = jnp.concatenate(rows, axis=0) - p[0:1, :]


def _lower_bounds(gamma):
    return pl.pallas_call(
        _lower_bounds_kernel,
        out_shape=jax.ShapeDtypeStruct(gamma.shape, F32),
        name="hgrn_lower_bounds",
    )(gamma)


def _hgrn_kernel(q_ref, f_ref, i_ref, g_ref, lb_ref, ng_ref, o_ref, s_ref):
    @pl.when(pl.program_id(2) == 0)
    def _():
        s_ref[...] = jnp.zeros_like(s_ref)

    lc, d = q_ref.shape
    q = _silu(q_ref[...].astype(F32))
    lb = lb_ref[...]
    la = jnp.log(lb)
    b_ = jnp.log1p(-lb) + _log_sigmoid(f_ref[...].astype(F32))
    log_f = jnp.maximum(la, b_) + jnp.log1p(jnp.exp(-jnp.abs(la - b_)))
    k = -_expm1(log_f)
    vb = i_ref[...]
    v = vb.astype(F32)
    c = log_f
    sh = 1
    while sh < lc:
        c = c + _shift_rows(c, sh, 0.0)
        sh *= 2
    c = c * LOG2_E
    tot = c[lc - 1:lc, :]

    s_prev = s_ref[...]
    o = _dot((q * jnp.exp2(c)).astype(BF16), s_prev.astype(BF16))
    row = _iota((lc, lc), 0)
    col = _iota((lc, lc), 1)
    x = jnp.where(row > col, row ^ col, 0)
    amat = jnp.zeros((lc, lc), F32)
    b = HG_BASE
    while b < lc:
        rho = jnp.broadcast_to(c.reshape(lc // (2 * b), 2 * b, d)[:, b - 1:b, :],
                               (lc // (2 * b), 2 * b, d)).reshape(lc, d)
        qq = (q * jnp.exp2(c - rho)).astype(BF16)
        kk = (k * jnp.exp2(rho - c)).astype(BF16)
        amat = jnp.where(jnp.logical_and(x >= b, x < 2 * b), _dot_nt(qq, kk), amat)
        b *= 2
    o = o + _dot(amat.astype(BF16), vb)
    nb = lc // HG_BASE
    q3 = q.reshape(nb, HG_BASE, d)
    k3 = k.reshape(nb, HG_BASE, d)
    c3 = c.reshape(nb, HG_BASE, d)
    v3 = v.reshape(nb, HG_BASE, d)
    tpos = _iota((nb, HG_BASE, d), 1)
    od = jnp.zeros((nb, HG_BASE, d), F32)
    for s in range(HG_BASE):
        term = q3 * k3[:, s:s + 1, :] * jnp.exp2(c3 - c3[:, s:s + 1, :])
        a_s = jnp.sum(jnp.where(tpos >= s, term, 0.0), axis=2, keepdims=True)
        od = od + a_s * v3[:, s:s + 1, :]
    o = o + od.reshape(lc, d)
    kd = (k * jnp.exp2(tot - c)).astype(BF16)
    s_ref[...] = _row_to_col(jnp.exp2(tot)) * s_prev + _dot_tn(kd, vb)
    on = o * lax.rsqrt(jnp.mean(o * o, axis=1, keepdims=True) + RMS_EPS) * ng_ref[...]
    o_ref[...] = (on * _silu(g_ref[...].astype(F32))).astype(o_ref.dtype)


def _hgrn_cell(p, lb, norm_g, batch, seq):
    heads, d = norm_g.shape
    lc = min(HG_CHUNK, seq)
    return pl.pallas_call(
        _hgrn_kernel,
        grid=(batch, heads, seq // lc),
        in_specs=[
            pl.BlockSpec((None, lc, d), lambda b, h, c: (b, c, h)),
            pl.BlockSpec((None, lc, d), lambda b, h, c: (b, c, heads + h)),
            pl.BlockSpec((None, lc, d), lambda b, h, c: (b, c, 2 * heads + h)),
            pl.BlockSpec((None, lc, d), lambda b, h, c: (b, c, 3 * heads + h)),
            pl.BlockSpec((1, d), lambda b, h, c: (0, h)),
            pl.BlockSpec((None, 1, d), lambda b, h, c: (h, 0, 0)),
        ],
        out_specs=pl.BlockSpec((None, lc, d), lambda b, h, c: (b, c, h)),
        out_shape=jax.ShapeDtypeStruct((batch, seq, heads * d), BF16),
        scratch_shapes=[pltpu.VMEM((d, d), F32)],
        compiler_params=_cparams(("parallel", "parallel", "arbitrary")),
        name="hgrn_cell",
    )(p, p, p, p, lb.reshape(1, heads * d), norm_g.reshape(heads, 1, d))


def _hgrn_mixer(hb, w_in, lb, norm_g, w_out, batch, seq):
    t = batch * seq
    d_model = w_out.shape[0]
    p = _matmul(hb, w_in.astype(BF16), BF16)
    cell = _hgrn_cell(p.reshape(batch, seq, 4 * d_model), lb, norm_g, batch, seq)
    return _matmul(cell.reshape(t, d_model), w_out.astype(BF16), F32)


def _router_kernel(w_ref, x_ref, b_ref, idx_ref, wt_ref):
    logits = _dot_nt(w_ref[...], x_ref[...])
    n_e, bt = logits.shape
    gs = n_e // N_GROUPS
    scores = _sigmoid(logits)
    sel = scores + b_ref[...]
    sel3 = sel.reshape(N_GROUPS, gs, bt)
    pos3 = _iota(sel3.shape, 1)
    m1 = jnp.max(sel3, axis=1, keepdims=True)
    i1 = jnp.min(jnp.where(sel3 == m1, pos3, gs), axis=1, keepdims=True)
    m2 = jnp.max(jnp.where(pos3 == i1, -jnp.inf, sel3), axis=1, keepdims=True)
    gscore = (m1 + m2).reshape(N_GROUPS, bt)
    gid = _iota(gscore.shape, 0)
    gmask = jnp.zeros(gscore.shape, F32)
    for _ in range(TOPK_GROUPS):
        gm = jnp.max(gscore, axis=0, keepdims=True)
        gi = jnp.min(jnp.where(gscore == gm, gid, N_GROUPS), axis=0, keepdims=True)
        hit = gid == gi
        gmask = jnp.where(hit, 1.0, gmask)
        gscore = jnp.where(hit, -jnp.inf, gscore)
    emask = jnp.broadcast_to(gmask.reshape(N_GROUPS, 1, bt), (N_GROUPS, gs, bt)).reshape(n_e, bt)
    cand = jnp.where(emask > 0.5, sel, -jnp.inf)
    eid = _iota(cand.shape, 0)
    idx_rows, w_rows = [], []
    for _ in range(TOP_K):
        cm = jnp.max(cand, axis=0, keepdims=True)
        ci = jnp.min(jnp.where(cand == cm, eid, n_e), axis=0, keepdims=True)
        hit = eid == ci
        idx_rows.append(ci)
        w_rows.append(jnp.sum(jnp.where(hit, scores, 0.0), axis=0, keepdims=True))
        cand = jnp.where(hit, -jnp.inf, cand)
    w = jnp.concatenate(w_rows, axis=0)
    w = w / jnp.sum(w, axis=0, keepdims=True) * ROUTED_SCALE
    idx_ref[...] = jnp.concatenate(idx_rows, axis=0)
    eye = (_iota((bt, bt), 0) == _iota((bt, bt), 1)).astype(BF16)
    w1 = w.astype(BF16)
    r1 = w - w1.astype(F32)
    w2 = r1.astype(BF16)
    w3 = (r1 - w2.astype(F32)).astype(BF16)
    wt_ref[...] = _dot_nt(eye, w1) + _dot_nt(eye, w2) + _dot_nt(eye, w3)


def _router(hb, w_router_t, bias_col, bt=512):
    t, d = hb.shape
    n_e = w_router_t.shape[0]
    bt = min(bt, t)
    return pl.pallas_call(
        _router_kernel,
        grid=(t // bt,),
        in_specs=[pl.BlockSpec((n_e, d), lambda i: (0, 0)),
                  pl.BlockSpec((bt, d), lambda i: (i, 0)),
                  pl.BlockSpec((n_e, 1), lambda i: (0, 0))],
        out_specs=[pl.BlockSpec((TOP_K, bt), lambda i: (0, i)),
                   pl.BlockSpec((bt, TOP_K), lambda i: (i, 0))],
        out_shape=[jax.ShapeDtypeStruct((TOP_K, t), I32), jax.ShapeDtypeStruct((t, TOP_K), F32)],
        compiler_params=_cparams(("parallel",)),
        name="moe_router",
    )(w_router_t, hb, bias_col)


def _plan_kernel(idx_ref, pos_ref, te_ref, tv_ref, *, n_e, tm, bt):
    k_top, t = idx_ref.shape
    n_steps = t // bt
    eid = _iota((n_e, bt), 0)
    upper = (_iota((bt, bt), 0) <= _iota((bt, bt), 1)).astype(BF16)

    def onehots(j):
        idx = idx_ref[:, pl.ds(pl.multiple_of(j * bt, bt), bt)]
        return [eid == idx[k:k + 1, :] for k in range(k_top)]

    def rank_body(j, carry):
        hits = onehots(j)
        m = jnp.zeros((n_e, bt), F32)
        for hmask in hits:
            m = m + hmask.astype(F32)
        cum = _dot(m.astype(BF16), upper)
        rank = carry + cum - m
        rows = [jnp.sum(jnp.where(hmask, rank, 0.0), axis=0, keepdims=True) for hmask in hits]
        pos_ref[:, pl.ds(pl.multiple_of(j * bt, bt), bt)] = jnp.concatenate(rows, axis=0).astype(I32)
        return carry + cum[:, bt - 1:bt]

    counts = lax.fori_loop(0, n_steps, rank_body, jnp.zeros((n_e, 1), F32))
    shift = tm.bit_length() - 1
    padded = (((counts.astype(I32) + (tm - 1)) >> shift) << shift).astype(F32)
    padded_row = _col_to_row(padded)
    below = _iota((n_e, n_e), 1) < _iota((n_e, n_e), 0)
    off = jnp.sum(jnp.where(below, padded_row, 0.0), axis=1, keepdims=True)

    def off_body(j, carry):
        hits = onehots(j)
        rows = [jnp.sum(jnp.where(hmask, off, 0.0), axis=0, keepdims=True) for hmask in hits]
        sl = pl.ds(pl.multiple_of(j * bt, bt), bt)
        pos_ref[:, sl] = pos_ref[:, sl] + jnp.concatenate(rows, axis=0).astype(I32)
        return carry

    lax.fori_loop(0, n_steps, off_body, 0)
    n_tiles = te_ref.shape[1]
    start = (_iota((n_e, n_tiles), 1) * tm).astype(F32)
    inside = jnp.logical_and(start >= off, start < off + padded)
    ecol = _iota((n_e, n_tiles), 0).astype(F32)
    te = jnp.sum(jnp.where(inside, ecol, 0.0), axis=0, keepdims=True)
    valid = jnp.sum(inside.astype(F32), axis=0, keepdims=True)
    last_e = jnp.max(jnp.where(padded > 0.0, ecol[:, 0:1], 0.0), axis=0, keepdims=True)
    te_ref[...] = jnp.where(valid > 0.0, te, last_e).astype(I32)
    tv_ref[...] = valid.astype(I32)


def _plan(idx, n_e, tm, n_tiles):
    k_top, t = idx.shape
    bt = min(512, t)
    return pl.pallas_call(
        functools.partial(_plan_kernel, n_e=n_e, tm=tm, bt=bt),
        out_shape=[jax.ShapeDtypeStruct((k_top, t), I32),
                   jax.ShapeDtypeStruct((1, n_tiles), I32),
                   jax.ShapeDtypeStruct((1, n_tiles), I32)],
        compiler_params=pltpu.CompilerParams(vmem_limit_bytes=VMEM_LIMIT_MID),
        name="moe_plan",
    )(idx)


def _dispatch_kernel(pos_ref, x_ref, xs_hbm, sem):
    k_top, bt = pos_ref.shape

    def start_body(t, carry):
        for k in range(k_top):
            pltpu.make_async_copy(x_ref.at[pl.ds(t, 1)], xs_hbm.at[pl.ds(pos_ref[k, t], 1)], sem).start()
        return carry

    lax.fori_loop(0, bt, start_body, 0, unroll=DMA_UNROLL)
    for k in range(k_top):
        pltpu.make_async_copy(x_ref, xs_hbm.at[pl.ds(0, bt)], sem).wait()


def _dispatch(pos, x_packed, n_rows):
    k_top, t = pos.shape
    half = x_packed.shape[1]
    bt = min(DISPATCH_TT, t)
    return pl.pallas_call(
        _dispatch_kernel,
        grid=(t // bt,),
        in_specs=[pl.BlockSpec((k_top, bt), lambda i: (0, i), memory_space=pltpu.SMEM),
                  pl.BlockSpec((bt, half), lambda i: (i, 0))],
        out_specs=pl.BlockSpec(memory_space=pl.ANY),
        out_shape=jax.ShapeDtypeStruct((n_rows, half), U32),
        scratch_shapes=[pltpu.SemaphoreType.DMA(())],
        compiler_params=_cparams(("arbitrary",)),
        name="moe_dispatch",
    )(pos, x_packed)


def _experts_kernel(te_ref, tv_ref, x_ref, wg_ref, wu_ref, wd_ref, o_ref, wgb_ref, wub_ref, wdb_ref):
    i = pl.program_id(0)

    @pl.when(jnp.logical_or(i == 0, te_ref[i] != te_ref[jnp.maximum(i - 1, 0)]))
    def _():
        wgb_ref[...] = wg_ref[...].astype(BF16)
        wub_ref[...] = wu_ref[...].astype(BF16)
        wdb_ref[...] = wd_ref[...].astype(BF16)

    @pl.when(tv_ref[i] > 0)
    def _():
        half = x_ref.shape[1]
        lo, hi = _unpack_bf16_pair(x_ref[...])
        xl = lo.astype(BF16)
        xh = hi.astype(BF16)
        g = _dot(xl, wgb_ref[:half, :]) + _dot(xh, wgb_ref[half:, :])
        u = _dot(xl, wub_ref[:half, :]) + _dot(xh, wub_ref[half:, :])
        hmid = (_silu(g) * u).astype(BF16)
        y = _dot(hmid, wdb_ref[...])
        o_ref[...] = _pack_bf16_pair(y[:, :half], y[:, half:])


def _experts(xs, tile_expert, tile_valid, w_gate, w_up, w_down, layer, tm):
    n_rows, half = xs.shape
    _, n_e, d, ff = w_gate.shape
    n_tiles = n_rows // tm
    wmap = lambda i, te, tv: (layer, te[i], 0, 0)
    return pl.pallas_call(
        _experts_kernel,
        grid_spec=pltpu.PrefetchScalarGridSpec(
            num_scalar_prefetch=2,
            grid=(n_tiles,),
            in_specs=[pl.BlockSpec((tm, half), lambda i, te, tv: (i, 0)),
                      pl.BlockSpec((None, None, d, ff), wmap),
                      pl.BlockSpec((None, None, d, ff), wmap),
                      pl.BlockSpec((None, None, ff, d), wmap)],
            out_specs=pl.BlockSpec((tm, half), lambda i, te, tv: (i, 0)),
            scratch_shapes=[pltpu.VMEM((d, ff), BF16), pltpu.VMEM((d, ff), BF16),
                            pltpu.VMEM((ff, d), BF16)],
        ),
        out_shape=jax.ShapeDtypeStruct((n_rows, half), U32),
        compiler_params=_cparams(("arbitrary",), VMEM_LIMIT_BIG),
        name="moe_experts",
    )(tile_expert, tile_valid, xs, w_gate, w_up, w_down)


def _combine_kernel(pos_ref, wt_ref, h_ref, wsg_ref, wsu_ref, wsd_ref, g_ref, b_ref, y_hbm,
                    of_ref, ob_ref, ybuf, sem, wsgb_ref, wsub_ref, wsdb_ref, *, alpha):
    k_top, bt = pos_ref.shape

    def start_body(t, carry):
        for k in range(k_top):
            pltpu.make_async_copy(y_hbm.at[pl.ds(pos_ref[k, t], 1)], ybuf.at[k, pl.ds(t, 1)], sem).start()
        return carry

    lax.fori_loop(0, bt, start_body, 0, unroll=DMA_UNROLL)

    @pl.when(pl.program_id(0) == 0)
    def _():
        wsgb_ref[...] = wsg_ref[...].astype(BF16)
        wsub_ref[...] = wsu_ref[...].astype(BF16)
        wsdb_ref[...] = wsd_ref[...].astype(BF16)

    h = h_ref[...]
    xb = h.astype(BF16)
    mid = (_silu(_dot(xb, wsgb_ref[...])) * _dot(xb, wsub_ref[...])).astype(BF16)
    of_ref[...] = alpha * h + _dot(mid, wsdb_ref[...])
    half = h.shape[1] // 2

    for k in range(k_top):
        pltpu.make_async_copy(y_hbm.at[pl.ds(0, bt)], ybuf.at[k], sem).wait()

    wt = wt_ref[...]
    ch = min(COMBINE_CHUNK, half)
    wb = [jnp.broadcast_to(wt[:, k:k + 1], (bt, ch)) for k in range(k_top)]
    for c in range(half // ch):
        lo_sl = slice(c * ch, (c + 1) * ch)
        hi_sl = slice(half + c * ch, half + (c + 1) * ch)
        acc_lo = of_ref[:, lo_sl]
        acc_hi = of_ref[:, hi_sl]
        for k in range(k_top):
            lo, hi = _unpack_bf16_pair(ybuf[k, :, lo_sl])
            acc_lo = acc_lo + wb[k] * lo
            acc_hi = acc_hi + wb[k] * hi
        of_ref[:, lo_sl] = acc_lo
        of_ref[:, hi_sl] = acc_hi
    y = _ln_math(of_ref[...], g_ref[...], b_ref[...])
    of_ref[...] = y
    ob_ref[...] = y.astype(BF16)


def _combine(pos, wt, h, ws_gate, ws_up, ws_down, g, b, y, layer, alpha):
    k_top, t = pos.shape
    d = h.shape[1]
    ff = ws_gate.shape[2]
    bt = min(COMBINE_TT, t)
    row = lambda i: (i, 0)
    const = lambda i: (0, 0)
    wmap = lambda i: (layer, 0, 0)
    return pl.pallas_call(
        functools.partial(_combine_kernel, alpha=alpha),
        grid=(t // bt,),
        in_specs=[pl.BlockSpec((k_top, bt), lambda i: (0, i), memory_space=pltpu.SMEM),
                  pl.BlockSpec((bt, k_top), row),
                  pl.BlockSpec((bt, d), row),
                  pl.BlockSpec((None, d, ff), wmap), pl.BlockSpec((None, d, ff), wmap),
                  pl.BlockSpec((None, ff, d), wmap),
                  pl.BlockSpec((1, d), const), pl.BlockSpec((1, d), const),
                  pl.BlockSpec(memory_space=pl.ANY)],
        out_specs=[pl.BlockSpec((bt, d), row), pl.BlockSpec((bt, d), row)],
        out_shape=[jax.ShapeDtypeStruct((t, d), F32), jax.ShapeDtypeStruct((t, d), BF16)],
        scratch_shapes=[pltpu.VMEM((k_top, bt, d // 2), U32), pltpu.SemaphoreType.DMA(()),
                        pltpu.VMEM((d, ff), BF16), pltpu.VMEM((d, ff), BF16), pltpu.VMEM((ff, d), BF16)],
        compiler_params=_cparams(("arbitrary",), VMEM_LIMIT_BIG),
        name="moe_combine",
    )(pos, wt, h, ws_gate, ws_up, ws_down, g.reshape(1, d), b.reshape(1, d), y)


def _moe_layer(h, hb, hu, w_router, router_bias, w_gate, w_up, w_down, ws_gate, ws_up, ws_down,
               ln_g, ln_b, layer, alpha):
    t, d = h.shape
    n_e = w_router.shape[1]
    tm = MOE_TM
    n_rows = t * TOP_K + n_e * tm
    n_tiles = n_rows // tm
    idx, wt = _router(hb, w_router.T.astype(BF16), router_bias.reshape(n_e, 1))
    pos, tile_expert, tile_valid = _plan(idx, n_e, tm, n_tiles)
    xs = _dispatch(pos, hu, n_rows)
    y = _experts(xs, tile_expert.reshape(n_tiles), tile_valid.reshape(n_tiles),
                 w_gate, w_up, w_down, layer, tm)
    return _combine(pos, wt, h, ws_gate, ws_up, ws_down, ln_g, ln_b, y, layer, alpha)


def kernel(x, ml_w_in, ml_b_gates, ml_norm_g, ml_w_out, rg_w_in, rg_conv_w, rg_conv_b, rg_w_ga, rg_b_ga, rg_w_gx, rg_b_gx, rg_lambda, rg_w_out, fx_w_in, fx_b_f, fx_w_out, hg_w_in, hg_gamma, hg_norm_g, hg_w_out, ln1_g, ln1_b, ln2_g, ln2_b, moe_w_router, moe_router_bias, moe_w_gate, moe_w_up, moe_w_down, moe_ws_gate, moe_ws_up, moe_ws_down):
    batch, seq, d = x.shape
    depth = ln1_g.shape[0]
    alpha = float((2 * depth) ** 0.25)
    t = batch * seq
    lbs = _lower_bounds(hg_gamma)
    h = x.reshape(t, d)
    hb = h.astype(BF16)
    for layer in range(depth):
        kind, j = layer % 4, layer // 4
        if kind == 0:
            mix = _mlstm_mixer(hb, ml_w_in[j], ml_b_gates[j], ml_norm_g[j], ml_w_out[j], batch, seq)
        elif kind == 1:
            mix = _rglru_mixer(hb, rg_w_in[j], rg_conv_w[j], rg_conv_b[j], rg_w_ga[j], rg_b_ga[j],
                               rg_w_gx[j], rg_b_gx[j], rg_lambda[j], rg_w_out[j], batch, seq)
        elif kind == 2:
            mix = _fox_mixer(hb, fx_w_in[j], fx_b_f[j], fx_w_out[j], batch, seq)
        else:
            mix = _hgrn_mixer(hb, hg_w_in[j], lbs[layer], hg_norm_g[j], hg_w_out[j], batch, seq)
        h, hb, hu = _residual_ln(h, mix, ln1_g[layer], ln1_b[layer], alpha)
        h, hb = _moe_layer(h, hb, hu, moe_w_router[layer], moe_router_bias[layer], moe_w_gate,
                           moe_w_up, moe_w_down, moe_ws_gate, moe_ws_up, moe_ws_down,
                           ln2_g[layer], ln2_b[layer], layer, alpha)
    return h.reshape(batch, seq, d)
```

```python
import functools

import jax
import jax.numpy as jnp
from jax import lax
from jax.experimental import pallas as pl
from jax.experimental.pallas import tpu as pltpu

F32 = jnp.float32
BF16 = jnp.bfloat16
I32 = jnp.int32
U32 = jnp.uint32

V7X_VMEM_BYTES = 64 * 1024 * 1024
VMEM_LIMIT_BIG = V7X_VMEM_BYTES - 8 * 1024 * 1024
VMEM_LIMIT_MID = 40 * 1024 * 1024

LN_EPS = 1e-5
RMS_EPS = 1e-6
RG_C = 8.0
ROUTED_SCALE = 2.5
N_GROUPS = 8
TOPK_GROUPS = 4
TOP_K = 8

ML_CHUNK = 64
ML_BLOCK = 512
ML_HEADS_PER_STEP = 4
RG_TILE = 256
FX_TQ = 512
FX_TK = 512
FX_HEADS_PER_STEP = 4
HG_CHUNK = 256
HG_BASE = 8
MOE_TM = 512
DISPATCH_TT = 512
COMBINE_TT = 128
COMBINE_CHUNK = 256
DMA_UNROLL = 4
NEG_BIG = -1e30
LOG2_E = 1.4426950408889634


def _cparams(sem, vmem=VMEM_LIMIT_MID):
    return pltpu.CompilerParams(dimension_semantics=sem, vmem_limit_bytes=vmem)


def _sigmoid(x):
    return 0.5 * jnp.tanh(0.5 * x) + 0.5


def _silu(x):
    return x * _sigmoid(x)


def _log_sigmoid(x):
    return jnp.minimum(x, 0.0) - jnp.log1p(jnp.exp(-jnp.abs(x)))


def _softplus(x):
    return jnp.maximum(x, 0.0) + jnp.log1p(jnp.exp(-jnp.abs(x)))


def _expm1(x):
    th = jnp.tanh(0.5 * x)
    return 2.0 * th / (1.0 - th)


def _gelu_tanh(x):
    return 0.5 * x * (1.0 + jnp.tanh(0.7978845608028654 * (x + 0.044715 * x * x * x)))


def _dot(a, b):
    return jnp.dot(a, b, preferred_element_type=F32)


def _dot_nt(a, b):
    return lax.dot_general(a, b, (((1,), (1,)), ((), ())), preferred_element_type=F32)


def _dot_tn(a, b):
    return lax.dot_general(a, b, (((0,), (0,)), ((), ())), preferred_element_type=F32)


def _iota(shape, dim):
    return lax.broadcasted_iota(I32, shape, dim)


def _row_to_col(row):
    n = row.shape[1]
    eye = _iota((n, n), 0) == _iota((n, n), 1)
    return jnp.sum(jnp.where(eye, row, 0.0), axis=1, keepdims=True)


def _col_to_row(col):
    n = col.shape[0]
    eye = _iota((n, n), 0) == _iota((n, n), 1)
    return jnp.sum(jnp.where(eye, col, 0.0), axis=0, keepdims=True)


def _shift_rows(x, d, fill):
    rolled = pltpu.roll(x, d, 0)
    return jnp.where(_iota(x.shape, 0) >= d, rolled, fill)


def _pack_bf16_pair(lo, hi):
    lo_b = lax.bitcast_convert_type(lo.astype(BF16).astype(F32), U32) >> 16
    hi_b = lax.bitcast_convert_type(hi.astype(BF16).astype(F32), U32)
    return lo_b | hi_b


def _unpack_bf16_pair(w):
    lo = lax.bitcast_convert_type(w << 16, F32)
    hi = lax.bitcast_convert_type(w & jnp.uint32(0xFFFF0000), F32)
    return lo, hi


def _mm_kernel(x_ref, w_ref, o_ref):
    o_ref[...] = _dot(x_ref[...], w_ref[...]).astype(o_ref.dtype)


def _matmul(x, w, out_dtype, n=None, bm=1024, bn=1024):
    m, k = x.shape
    n = w.shape[1] if n is None else n
    bm, bn = min(bm, m), min(bn, n)
    assert m % bm == 0 and n % bn == 0
    return pl.pallas_call(
        _mm_kernel,
        grid=(n // bn, m // bm),
        in_specs=[pl.BlockSpec((bm, k), lambda j, i: (i, 0)),
                  pl.BlockSpec((k, bn), lambda j, i: (0, j))],
        out_specs=pl.BlockSpec((bm, bn), lambda j, i: (i, j)),
        out_shape=jax.ShapeDtypeStruct((m, n), out_dtype),
        compiler_params=_cparams(("parallel", "parallel"), VMEM_LIMIT_BIG),
        name="matmul",
    )(x, w)


def _gates_kernel(w_ref, x_ref, b_ref, o_ref, *, ls_from):
    z = _dot_nt(w_ref[...], x_ref[...]) + b_ref[...]
    row = _iota(z.shape, 0)
    o_ref[...] = jnp.where(row >= ls_from, _log_sigmoid(z), z)


def _gates_t(x, w_t, bias_col, ls_from, bt=1024):
    t, d = x.shape
    g = w_t.shape[0]
    bt = min(bt, t)
    return pl.pallas_call(
        functools.partial(_gates_kernel, ls_from=ls_from),
        grid=(t // bt,),
        in_specs=[pl.BlockSpec((g, d), lambda i: (0, 0)),
                  pl.BlockSpec((bt, d), lambda i: (i, 0)),
                  pl.BlockSpec((g, 1), lambda i: (0, 0))],
        out_specs=pl.BlockSpec((g, bt), lambda i: (0, i)),
        out_shape=jax.ShapeDtypeStruct((g, t), F32),
        compiler_params=_cparams(("parallel",)),
        name="gates_t",
    )(w_t, x, bias_col)


def _ln_math(x, g, b):
    mu = jnp.mean(x, axis=-1, keepdims=True)
    xc = x - mu
    var = jnp.mean(xc * xc, axis=-1, keepdims=True)
    return xc * lax.rsqrt(var + LN_EPS) * g + b


def _ln_kernel(h_ref, m_ref, g_ref, b_ref, of_ref, ob_ref, ou_ref, *, alpha):
    y = _ln_math(alpha * h_ref[...] + m_ref[...].astype(F32), g_ref[...], b_ref[...])
    of_ref[...] = y
    ob_ref[...] = y.astype(BF16)
    half = y.shape[1] // 2
    ou_ref[...] = _pack_bf16_pair(y[:, :half], y[:, half:])


def _residual_ln(h, mix, g, b, alpha, bt=128):
    t, d = h.shape
    bt = min(bt, t)
    row = lambda i: (i, 0)
    return pl.pallas_call(
        functools.partial(_ln_kernel, alpha=alpha),
        grid=(t // bt,),
        in_specs=[pl.BlockSpec((bt, d), row), pl.BlockSpec((bt, d), row),
                  pl.BlockSpec((1, d), lambda i: (0, 0)), pl.BlockSpec((1, d), lambda i: (0, 0))],
        out_specs=[pl.BlockSpec((bt, d), row), pl.BlockSpec((bt, d), row),
                   pl.BlockSpec((bt, d // 2), row)],
        out_shape=[jax.ShapeDtypeStruct((t, d), F32), jax.ShapeDtypeStruct((t, d), BF16),
                   jax.ShapeDtypeStruct((t, d // 2), U32)],
        compiler_params=_cparams(("parallel",)),
        name="residual_ln",
    )(h, mix, g.reshape(1, d), b.reshape(1, d))


def _mlstm_kernel(q_ref, k_ref, v_ref, og_ref, gi_ref, gf_ref, ng_ref, o_ref,
                  c_ref, n_ref, m_ref, *, chunk, scale, dk, dv):
    @pl.when(pl.program_id(2) == 0)
    def _():
        c_ref[...] = jnp.zeros_like(c_ref)
        n_ref[...] = jnp.zeros_like(n_ref)
        m_ref[...] = jnp.zeros_like(m_ref)

    L = chunk
    n_chunks = q_ref.shape[0] // L
    nh = q_ref.shape[1] // dk
    row = _iota((L, L), 0)
    col = _iota((L, L), 1)
    tri = row >= col

    def head_step(a, j, r0):
        q = q_ref[pl.ds(r0, L), a * dk:(a + 1) * dk]
        k = k_ref[pl.ds(r0, L), a * dk:(a + 1) * dk]
        v = v_ref[pl.ds(r0, L), a * dv:(a + 1) * dv]
        li_row = gi_ref[a, pl.ds(j, 1), :]
        lf_row = gf_ref[a, pl.ds(j, 1), :]
        lf_col = _row_to_col(lf_row)
        g_col = jnp.sum(jnp.where(tri, lf_row, 0.0), axis=1, keepdims=True)
        g_row = jnp.sum(jnp.where(row <= col, lf_col, 0.0), axis=0, keepdims=True)
        a_row = li_row - g_row
        a_col = _row_to_col(a_row)
        amax_col = jnp.max(jnp.where(tri, a_row, NEG_BIG), axis=1, keepdims=True)
        m_prev = m_ref[a]
        mm_col = jnp.maximum(amax_col, m_prev)
        e = jnp.where(tri, jnp.exp(jnp.minimum(a_row - mm_col, 0.0)), 0.0)
        w_inter = jnp.exp(m_prev - mm_col)
        qs = (q.astype(F32) * scale).astype(BF16)
        s = _dot_nt(qs, k) * e
        c_prev = c_ref[a]
        num = w_inter * _dot(qs, c_prev.astype(BF16)) + _dot(s.astype(BF16), v)
        qn = jnp.sum(qs.astype(F32) * n_ref[a], axis=1, keepdims=True)
        den = w_inter * qn + jnp.sum(s, axis=1, keepdims=True)
        m_t = g_col + mm_col
        h = num / jnp.maximum(jnp.abs(den), jnp.exp(-m_t))
        hn = h * lax.rsqrt(jnp.mean(h * h, axis=1, keepdims=True) + RMS_EPS) * ng_ref[a]
        og = og_ref[pl.ds(r0, L), a * dv:(a + 1) * dv].astype(F32)
        o_ref[pl.ds(r0, L), a * dv:(a + 1) * dv] = (hn * _sigmoid(og)).astype(o_ref.dtype)
        a_last = jnp.max(a_row, axis=1, keepdims=True)
        mm_last = jnp.maximum(a_last, m_prev)
        g_last = jnp.sum(lf_row, axis=1, keepdims=True)
        wu_col = jnp.exp(a_col - mm_last)
        decay = jnp.exp(m_prev - mm_last)
        wv = (wu_col * v.astype(F32)).astype(BF16)
        c_ref[a] = decay * c_prev + _dot_tn(k, wv)
        n_ref[a] = decay * n_ref[a] + jnp.sum(wu_col * k.astype(F32), axis=0, keepdims=True)
        m_ref[a] = g_last + mm_last

    def body(j, carry):
        r0 = pl.multiple_of(j * L, L)
        for a in range(nh):
            head_step(a, j, r0)
        return carry

    lax.fori_loop(0, n_chunks, body, 0)


def _mlstm_cell(p, gates_c, norm_g, batch, seq, heads, dk, dv):
    L = ML_CHUNK
    lb = min(ML_BLOCK, seq)
    nb = seq // lb
    cpb = lb // L
    nh = ML_HEADS_PER_STEP
    assert heads % nh == 0 and (2 * heads * dk) % (nh * dv) == 0
    hp = heads // nh
    v0 = (2 * heads * dk) // (nh * dv)
    kern = functools.partial(_mlstm_kernel, chunk=L, scale=float(dk) ** -0.5, dk=dk, dv=dv)
    return pl.pallas_call(
        kern,
        grid=(batch, hp, nb),
        in_specs=[
            pl.BlockSpec((None, lb, nh * dk), lambda b, h, c: (b, c, h)),
            pl.BlockSpec((None, lb, nh * dk), lambda b, h, c: (b, c, hp + h)),
            pl.BlockSpec((None, lb, nh * dv), lambda b, h, c: (b, c, v0 + h)),
            pl.BlockSpec((None, lb, nh * dv), lambda b, h, c: (b, c, v0 + hp + h)),
            pl.BlockSpec((nh, cpb, L), lambda b, h, c: (h, b * nb + c, 0)),
            pl.BlockSpec((nh, cpb, L), lambda b, h, c: (hp + h, b * nb + c, 0)),
            pl.BlockSpec((nh, 1, dv), lambda b, h, c: (h, 0, 0)),
        ],
        out_specs=pl.BlockSpec((None, lb, nh * dv), lambda b, h, c: (b, c, h)),
        out_shape=jax.ShapeDtypeStruct((batch, seq, heads * dv), BF16),
        scratch_shapes=[pltpu.VMEM((nh, dk, dv), F32), pltpu.VMEM((nh, 1, dk), F32),
                        pltpu.VMEM((nh, 1, 1), F32)],
        compiler_params=_cparams(("parallel", "parallel", "arbitrary")),
        name="mlstm_cell",
    )(p, p, p, p, gates_c, gates_c, norm_g.reshape(heads, 1, dv))


def _mlstm_mixer(hb, w_in, b_gates, norm_g, w_out, batch, seq):
    heads = b_gates.shape[0] // 2
    dv = norm_g.shape[1]
    dk = (w_in.shape[1] - 2 * heads - 2 * heads * dv) // (2 * heads)
    n_main = 2 * heads * dk + 2 * heads * dv
    t = batch * seq
    p = _matmul(hb, w_in.astype(BF16), BF16, n=n_main)
    w_g_t = w_in[:, n_main:].T.astype(BF16)
    gates = _gates_t(hb, w_g_t, b_gates.reshape(-1, 1), ls_from=heads)
    gates_c = gates.reshape(2 * heads, t // ML_CHUNK, ML_CHUNK)
    cell = _mlstm_cell(p.reshape(batch, seq, n_main), gates_c, norm_g, batch, seq, heads, dk, dv)
    return _matmul(cell.reshape(t, heads * dv), w_out.astype(BF16), F32)


def _rglru_kernel(y_ref, u_ref, cw_ref, cb_ref, wa_ref, ba_ref, wx_ref, bx_ref, lam_ref, o_ref,
                  tail_ref, hc_ref):
    c = pl.program_id(2)

    @pl.when(c == 0)
    def _():
        tail_ref[...] = jnp.zeros_like(tail_ref)
        hc_ref[...] = jnp.zeros_like(hc_ref)

    ts = u_ref.shape[0]
    u = u_ref[...].astype(F32)
    cw = cw_ref[...]
    taps = cw.shape[0]
    ext = jnp.concatenate([tail_ref[...], u], axis=0)
    uc = cw[taps - 1:taps, :] * u + cb_ref[...]
    for j in range(taps - 1):
        d = taps - 1 - j
        uc = uc + cw[j:j + 1, :] * ext[8 - d:8 - d + ts, :]
    tail_ref[...] = u[ts - 8:, :]

    ub = uc.astype(BF16)
    r = _sigmoid(_dot(ub, wa_ref[...]) + ba_ref[...])
    i = _sigmoid(_dot(ub, wx_ref[...]) + bx_ref[...])
    log_a = -RG_C * r * _softplus(-lam_ref[...])
    a = jnp.exp(log_a)
    mult = jnp.sqrt(-_expm1(2.0 * log_a))
    first = jnp.logical_and(c == 0, _iota(a.shape, 0) == 0)
    mult = jnp.where(first, 1.0, mult)
    bv = mult * i * uc
    d = 1
    while d < ts:
        a_sh = _shift_rows(a, d, 1.0)
        b_sh = _shift_rows(bv, d, 0.0)
        bv = a * b_sh + bv
        a = a * a_sh
        d *= 2
    h = bv + a * hc_ref[...]
    hc_ref[...] = h[ts - 1:ts, :]
    o_ref[...] = (h * _gelu_tanh(y_ref[...].astype(F32))).astype(o_ref.dtype)


def _rglru_cell(p, conv_w, conv_b, w_ga, b_ga, w_gx, b_gx, lam, batch, seq):
    heads, blk = b_ga.shape
    width = heads * blk
    ts = min(RG_TILE, seq)
    taps = conv_w.shape[0]
    hmap = lambda b, h, c: (h, 0, 0)
    cmap = lambda b, h, c: (0, h)
    return pl.pallas_call(
        _rglru_kernel,
        grid=(batch, heads, seq // ts),
        in_specs=[
            pl.BlockSpec((None, ts, blk), lambda b, h, c: (b, c, h)),
            pl.BlockSpec((None, ts, blk), lambda b, h, c: (b, c, heads + h)),
            pl.BlockSpec((taps, blk), cmap),
            pl.BlockSpec((1, blk), cmap),
            pl.BlockSpec((None, blk, blk), hmap),
            pl.BlockSpec((None, 1, blk), hmap),
            pl.BlockSpec((None, blk, blk), hmap),
            pl.BlockSpec((None, 1, blk), hmap),
            pl.BlockSpec((1, blk), cmap),
        ],
        out_specs=pl.BlockSpec((None, ts, blk), lambda b, h, c: (b, c, h)),
        out_shape=jax.ShapeDtypeStruct((batch, seq, width), BF16),
        scratch_shapes=[pltpu.VMEM((8, blk), F32), pltpu.VMEM((1, blk), F32)],
        compiler_params=_cparams(("parallel", "parallel", "arbitrary")),
        name="rglru_cell",
    )(p, p, conv_w, conv_b.reshape(1, width), w_ga.astype(BF16), b_ga.reshape(heads, 1, blk),
      w_gx.astype(BF16), b_gx.reshape(heads, 1, blk), lam.reshape(1, width))


def _rglru_mixer(hb, w_in, conv_w, conv_b, w_ga, b_ga, w_gx, b_gx, lam, w_out, batch, seq):
    t = batch * seq
    width = conv_b.shape[0]
    p = _matmul(hb, w_in.astype(BF16), BF16)
    cell = _rglru_cell(p.reshape(batch, seq, 2 * width), conv_w, conv_b, w_ga, b_ga, w_gx, b_gx,
                       lam, batch, seq)
    return _matmul(cell.reshape(t, width), w_out.astype(BF16), F32)


def _cumsum_lanes_kernel(x_ref, o_ref):
    x = x_ref[...]
    n = x.shape[1]
    lane = _iota(x.shape, 1) % 128
    d = 1
    while d < 128:
        x = x + jnp.where(lane >= d, pltpu.roll(x, d, 1), 0.0)
        d *= 2
    carry = jnp.zeros((x.shape[0], 1), F32)
    for b in range(n // 128):
        blk = x[:, b * 128:(b + 1) * 128] + carry
        o_ref[:, b * 128:(b + 1) * 128] = blk
        carry = blk[:, 127:128]


def _cumsum_lanes(x, seg):
    g, t = x.shape
    return pl.pallas_call(
        _cumsum_lanes_kernel,
        grid=(t // seg,),
        in_specs=[pl.BlockSpec((g, seg), lambda i: (0, i))],
        out_specs=pl.BlockSpec((g, seg), lambda i: (0, i)),
        out_shape=jax.ShapeDtypeStruct((g, t), F32),
        compiler_params=_cparams(("parallel",)),
        name="cumsum_lanes",
    )(x)


def _fox_kernel(q_ref, k_ref, v_ref, f_ref, o_ref, v1_ref, *, scale, hd, tk):
    qi = pl.program_id(2)
    tq = q_ref.shape[0]
    n_diag = tq // tk
    nh = q_ref.shape[1] // hd

    @pl.when(qi == 0)
    def _():
        for a in range(nh):
            v1_ref[a, :, :hd] = v_ref[:, a * hd:(a + 1) * hd]
            v1_ref[a, :, hd:] = jnp.ones((v_ref.shape[0], hd), v1_ref.dtype)

    qs = [(q_ref[:, a * hd:(a + 1) * hd].astype(F32) * scale).astype(BF16) for a in range(nh)]

    def step(j, carry, diag_off):
        c0 = pl.multiple_of(j * tk, tk)
        out = []
        for a in range(nh):
            m, acc = carry[a]
            kb = k_ref[pl.ds(c0, tk), a * hd:(a + 1) * hd]
            vb = v1_ref[a, pl.ds(c0, tk), :]
            fk = f_ref[a, :, pl.ds(c0, tk)]
            s = _dot_nt(qs[a], kb) - fk
            if diag_off is not None:
                s = jnp.where(_iota((tq, tk), 0) >= _iota((tq, tk), 1) + diag_off, s, NEG_BIG)
            m_new = jnp.maximum(m, jnp.max(s, axis=1, keepdims=True))
            p = jnp.exp((s - m_new).astype(BF16))
            out.append((m_new, jnp.exp(m - m_new) * acc + _dot(p, vb)))
        return tuple(out)

    init = tuple((jnp.full((tq, 1), NEG_BIG, F32), jnp.zeros((tq, 2 * hd), F32)) for _ in range(nh))
    final = lax.fori_loop(0, qi * n_diag, functools.partial(step, diag_off=None), init)
    for r in range(n_diag):
        final = step(qi * n_diag + r, final, r * tk)
    for a in range(nh):
        acc = final[a][1]
        o_ref[:, a * hd:(a + 1) * hd] = (acc[:, :hd] / acc[:, hd:hd + 1]).astype(o_ref.dtype)


def _fox_attention(p, f_cum, batch, seq, heads, hd):
    tq = min(FX_TQ, seq)
    tk = min(FX_TK, tq)
    assert tq % tk == 0
    nh = FX_HEADS_PER_STEP
    assert heads % nh == 0
    hp = heads // nh
    w = nh * hd
    return pl.pallas_call(
        functools.partial(_fox_kernel, scale=float(hd) ** -0.5, hd=hd, tk=tk),
        grid=(batch, hp, seq // tq),
        in_specs=[
            pl.BlockSpec((None, tq, w), lambda b, h, i: (b, i, h)),
            pl.BlockSpec((None, seq, w), lambda b, h, i: (b, 0, hp + h)),
            pl.BlockSpec((None, seq, w), lambda b, h, i: (b, 0, 2 * hp + h)),
            pl.BlockSpec((nh, 1, seq), lambda b, h, i: (h, 0, b)),
        ],
        out_specs=pl.BlockSpec((None, tq, w), lambda b, h, i: (b, i, h)),
        out_shape=jax.ShapeDtypeStruct((batch, seq, heads * hd), BF16),
        scratch_shapes=[pltpu.VMEM((nh, seq, 2 * hd), BF16)],
        compiler_params=_cparams(("parallel", "parallel", "arbitrary"), VMEM_LIMIT_BIG),
        name="fox_attention",
    )(p, p, p, f_cum)


def _fox_mixer(hb, w_in, b_f, w_out, batch, seq):
    heads = b_f.shape[0]
    d_model = w_out.shape[0]
    hd = d_model // heads
    t = batch * seq
    p = _matmul(hb, w_in.astype(BF16), BF16, n=3 * d_model)
    w_f_t = w_in[:, 3 * d_model:].T.astype(BF16)
    log_f = _gates_t(hb, w_f_t, b_f.reshape(-1, 1), ls_from=0)
    f_cum = _cumsum_lanes(log_f, seq).reshape(heads, 1, t)
    o = _fox_attention(p.reshape(batch, seq, 3 * d_model), f_cum, batch, seq, heads, hd)
    return _matmul(o.reshape(t, d_model), w_out.astype(BF16), F32)


def _lower_bounds_kernel(g_ref, o_ref):
    g = g_ref[...]
    e = jnp.exp(g - jnp.max(g, axis=0, keepdims=True))
    p = e / jnp.sum(e, axis=0, keepdims=True)
    layer = _iota(p.shape, 0)
    rows = [jnp.sum(jnp.where(layer <= i, p, 0.0), axis=0, keepdims=True) for i in range(g.shape[0])]
    o_ref[...] = jnp.concatenate(rows, axis=0) - p[0:1, :]


def _lower_bounds(gamma):
    return pl.pallas_call(
        _lower_bounds_kernel,
        out_shape=jax.ShapeDtypeStruct(gamma.shape, F32),
        name="hgrn_lower_bounds",
    )(gamma)


def _hgrn_kernel(q_ref, f_ref, i_ref, g_ref, lb_ref, ng_ref, o_ref, s_ref):
    @pl.when(pl.program_id(2) == 0)
    def _():
        s_ref[...] = jnp.zeros_like(s_ref)

    lc, d = q_ref.shape
    q = _silu(q_ref[...].astype(F32))
    lb = lb_ref[...]
    la = jnp.log(lb)
    b_ = jnp.log1p(-lb) + _log_sigmoid(f_ref[...].astype(F32))
    log_f = jnp.maximum(la, b_) + jnp.log1p(jnp.exp(-jnp.abs(la - b_)))
    k = -_expm1(log_f)
    vb = i_ref[...]
    v = vb.astype(F32)
    c = log_f
    sh = 1
    while sh < lc:
        c = c + _shift_rows(c, sh, 0.0)
        sh *= 2
    c = c * LOG2_E
    tot = c[lc - 1:lc, :]

    s_prev = s_ref[...]
    o = _dot((q * jnp.exp2(c)).astype(BF16), s_prev.astype(BF16))
    row = _iota((lc, lc), 0)
    col = _iota((lc, lc), 1)
    x = jnp.where(row > col, row ^ col, 0)
    amat = jnp.zeros((lc, lc), F32)
    b = HG_BASE
    while b < lc:
        rho = jnp.broadcast_to(c.reshape(lc // (2 * b), 2 * b, d)[:, b - 1:b, :],
                               (lc // (2 * b), 2 * b, d)).reshape(lc, d)
        qq = (q * jnp.exp2(c - rho)).astype(BF16)
        kk = (k * jnp.exp2(rho - c)).astype(BF16)
        amat = jnp.where(jnp.logical_and(x >= b, x < 2 * b), _dot_nt(qq, kk), amat)
        b *= 2
    o = o + _dot(amat.astype(BF16), vb)
    nb = lc // HG_BASE
    q3 = q.reshape(nb, HG_BASE, d)
    k3 = k.reshape(nb, HG_BASE, d)
    c3 = c.reshape(nb, HG_BASE, d)
    v3 = v.reshape(nb, HG_BASE, d)
    tpos = _iota((nb, HG_BASE, d), 1)
    od = jnp.zeros((nb, HG_BASE, d), F32)
    for s in range(HG_BASE):
        term = q3 * k3[:, s:s + 1, :] * jnp.exp2(c3 - c3[:, s:s + 1, :])
        a_s = jnp.sum(jnp.where(tpos >= s, term, 0.0), axis=2, keepdims=True)
        od = od + a_s * v3[:, s:s + 1, :]
    o = o + od.reshape(lc, d)
    kd = (k * jnp.exp2(tot - c)).astype(BF16)
    s_ref[...] = _row_to_col(jnp.exp2(tot)) * s_prev + _dot_tn(kd, vb)
    on = o * lax.rsqrt(jnp.mean(o * o, axis=1, keepdims=True) + RMS_EPS) * ng_ref[...]
    o_ref[...] = (on * _silu(g_ref[...].astype(F32))).astype(o_ref.dtype)


def _hgrn_cell(p, lb, norm_g, batch, seq):
    heads, d = norm_g.shape
    lc = min(HG_CHUNK, seq)
    return pl.pallas_call(
        _hgrn_kernel,
        grid=(batch, heads, seq // lc),
        in_specs=[
            pl.BlockSpec((None, lc, d), lambda b, h, c: (b, c, h)),
            pl.BlockSpec((None, lc, d), lambda b, h, c: (b, c, heads + h)),
            pl.BlockSpec((None, lc, d), lambda b, h, c: (b, c, 2 * heads + h)),
            pl.BlockSpec((None, lc, d), lambda b, h, c: (b, c, 3 * heads + h)),
            pl.BlockSpec((1, d), lambda b, h, c: (0, h)),
            pl.BlockSpec((None, 1, d), lambda b, h, c: (h, 0, 0)),
        ],
        out_specs=pl.BlockSpec((None, lc, d), lambda b, h, c: (b, c, h)),
        out_shape=jax.ShapeDtypeStruct((batch, seq, heads * d), BF16),
        scratch_shapes=[pltpu.VMEM((d, d), F32)],
        compiler_params=_cparams(("parallel", "parallel", "arbitrary")),
        name="hgrn_cell",
    )(p, p, p, p, lb.reshape(1, heads * d), norm_g.reshape(heads, 1, d))


def _hgrn_mixer(hb, w_in, lb, norm_g, w_out, batch, seq):
    t = batch * seq
    d_model = w_out.shape[0]
    p = _matmul(hb, w_in.astype(BF16), BF16)
    cell = _hgrn_cell(p.reshape(batch, seq, 4 * d_model), lb, norm_g, batch, seq)
    return _matmul(cell.reshape(t, d_model), w_out.astype(BF16), F32)


def _router_kernel(w_ref, x_ref, b_ref, idx_ref, wt_ref):
    logits = _dot_nt(w_ref[...], x_ref[...])
    n_e, bt = logits.shape
    gs = n_e // N_GROUPS
    scores = _sigmoid(logits)
    sel = scores + b_ref[...]
    sel3 = sel.reshape(N_GROUPS, gs, bt)
    pos3 = _iota(sel3.shape, 1)
    m1 = jnp.max(sel3, axis=1, keepdims=True)
    i1 = jnp.min(jnp.where(sel3 == m1, pos3, gs), axis=1, keepdims=True)
    m2 = jnp.max(jnp.where(pos3 == i1, -jnp.inf, sel3), axis=1, keepdims=True)
    gscore = (m1 + m2).reshape(N_GROUPS, bt)
    gid = _iota(gscore.shape, 0)
    gmask = jnp.zeros(gscore.shape, F32)
    for _ in range(TOPK_GROUPS):
        gm = jnp.max(gscore, axis=0, keepdims=True)
        gi = jnp.min(jnp.where(gscore == gm, gid, N_GROUPS), axis=0, keepdims=True)
        hit = gid == gi
        gmask = jnp.where(hit, 1.0, gmask)
        gscore = jnp.where(hit, -jnp.inf, gscore)
    emask = jnp.broadcast_to(gmask.reshape(N_GROUPS, 1, bt), (N_GROUPS, gs, bt)).reshape(n_e, bt)
    cand = jnp.where(emask > 0.5, sel, -jnp.inf)
    eid = _iota(cand.shape, 0)
    idx_rows, w_rows = [], []
    for _ in range(TOP_K):
        cm = jnp.max(cand, axis=0, keepdims=True)
        ci = jnp.min(jnp.where(cand == cm, eid, n_e), axis=0, keepdims=True)
        hit = eid == ci
        idx_rows.append(ci)
        w_rows.append(jnp.sum(jnp.where(hit, scores, 0.0), axis=0, keepdims=True))
        cand = jnp.where(hit, -jnp.inf, cand)
    w = jnp.concatenate(w_rows, axis=0)
    w = w / jnp.sum(w, axis=0, keepdims=True) * ROUTED_SCALE
    idx_ref[...] = jnp.concatenate(idx_rows, axis=0)
    eye = (_iota((bt, bt), 0) == _iota((bt, bt), 1)).astype(BF16)
    w1 = w.astype(BF16)
    r1 = w - w1.astype(F32)
    w2 = r1.astype(BF16)
    w3 = (r1 - w2.astype(F32)).astype(BF16)
    wt_ref[...] = _dot_nt(eye, w1) + _dot_nt(eye, w2) + _dot_nt(eye, w3)


def _router(hb, w_router_t, bias_col, bt=512):
    t, d = hb.shape
    n_e = w_router_t.shape[0]
    bt = min(bt, t)
    return pl.pallas_call(
        _router_kernel,
        grid=(t // bt,),
        in_specs=[pl.BlockSpec((n_e, d), lambda i: (0, 0)),
                  pl.BlockSpec((bt, d), lambda i: (i, 0)),
                  pl.BlockSpec((n_e, 1), lambda i: (0, 0))],
        out_specs=[pl.BlockSpec((TOP_K, bt), lambda i: (0, i)),
                   pl.BlockSpec((bt, TOP_K), lambda i: (i, 0))],
        out_shape=[jax.ShapeDtypeStruct((TOP_K, t), I32), jax.ShapeDtypeStruct((t, TOP_K), F32)],
        compiler_params=_cparams(("parallel",)),
        name="moe_router",
    )(w_router_t, hb, bias_col)


def _plan_kernel(idx_ref, pos_ref, te_ref, tv_ref, *, n_e, tm, bt):
    k_top, t = idx_ref.shape
    n_steps = t // bt
    eid = _iota((n_e, bt), 0)
    upper = (_iota((bt, bt), 0) <= _iota((bt, bt), 1)).astype(BF16)

    def onehots(j):
        idx = idx_ref[:, pl.ds(pl.multiple_of(j * bt, bt), bt)]
        return [eid == idx[k:k + 1, :] for k in range(k_top)]

    def rank_body(j, carry):
        hits = onehots(j)
        m = jnp.zeros((n_e, bt), F32)
        for hmask in hits:
            m = m + hmask.astype(F32)
        cum = _dot(m.astype(BF16), upper)
        rank = carry + cum - m
        rows = [jnp.sum(jnp.where(hmask, rank, 0.0), axis=0, keepdims=True) for hmask in hits]
        pos_ref[:, pl.ds(pl.multiple_of(j * bt, bt), bt)] = jnp.concatenate(rows, axis=0).astype(I32)
        return carry + cum[:, bt - 1:bt]

    counts = lax.fori_loop(0, n_steps, rank_body, jnp.zeros((n_e, 1), F32))
    shift = tm.bit_length() - 1
    padded = (((counts.astype(I32) + (tm - 1)) >> shift) << shift).astype(F32)
    padded_row = _col_to_row(padded)
    below = _iota((n_e, n_e), 1) < _iota((n_e, n_e), 0)
    off = jnp.sum(jnp.where(below, padded_row, 0.0), axis=1, keepdims=True)

    def off_body(j, carry):
        hits = onehots(j)
        rows = [jnp.sum(jnp.where(hmask, off, 0.0), axis=0, keepdims=True) for hmask in hits]
        sl = pl.ds(pl.multiple_of(j * bt, bt), bt)
        pos_ref[:, sl] = pos_ref[:, sl] + jnp.concatenate(rows, axis=0).astype(I32)
        return carry

    lax.fori_loop(0, n_steps, off_body, 0)
    n_tiles = te_ref.shape[1]
    start = (_iota((n_e, n_tiles), 1) * tm).astype(F32)
    inside = jnp.logical_and(start >= off, start < off + padded)
    ecol = _iota((n_e, n_tiles), 0).astype(F32)
    te = jnp.sum(jnp.where(inside, ecol, 0.0), axis=0, keepdims=True)
    valid = jnp.sum(inside.astype(F32), axis=0, keepdims=True)
    last_e = jnp.max(jnp.where(padded > 0.0, ecol[:, 0:1], 0.0), axis=0, keepdims=True)
    te_ref[...] = jnp.where(valid > 0.0, te, last_e).astype(I32)
    tv_ref[...] = valid.astype(I32)


def _plan(idx, n_e, tm, n_tiles):
    k_top, t = idx.shape
    bt = min(512, t)
    return pl.pallas_call(
        functools.partial(_plan_kernel, n_e=n_e, tm=tm, bt=bt),
        out_shape=[jax.ShapeDtypeStruct((k_top, t), I32),
                   jax.ShapeDtypeStruct((1, n_tiles), I32),
                   jax.ShapeDtypeStruct((1, n_tiles), I32)],
        compiler_params=pltpu.CompilerParams(vmem_limit_bytes=VMEM_LIMIT_MID),
        name="moe_plan",
    )(idx)


def _dispatch_kernel(pos_ref, x_ref, xs_hbm, sem):
    k_top, bt = pos_ref.shape

    def start_body(t, carry):
        for k in range(k_top):
            pltpu.make_async_copy(x_ref.at[pl.ds(t, 1)], xs_hbm.at[pl.ds(pos_ref[k, t], 1)],
                                  sem).start(priority=k % 2)
        return carry

    lax.fori_loop(0, bt, start_body, 0, unroll=DMA_UNROLL)
    for k in range(k_top):
        pltpu.make_async_copy(x_ref, xs_hbm.at[pl.ds(0, bt)], sem).wait()


def _dispatch(pos, x_packed, n_rows):
    k_top, t = pos.shape
    half = x_packed.shape[1]
    bt = min(DISPATCH_TT, t)
    return pl.pallas_call(
        _dispatch_kernel,
        grid=(t // bt,),
        in_specs=[pl.BlockSpec((k_top, bt), lambda i: (0, i), memory_space=pltpu.SMEM),
                  pl.BlockSpec((bt, half), lambda i: (i, 0))],
        out_specs=pl.BlockSpec(memory_space=pl.ANY),
        out_shape=jax.ShapeDtypeStruct((n_rows, half), U32),
        scratch_shapes=[pltpu.SemaphoreType.DMA(())],
        compiler_params=_cparams(("arbitrary",)),
        name="moe_dispatch",
    )(pos, x_packed)


def _experts_kernel(te_ref, tv_ref, x_ref, wg_ref, wu_ref, wd_ref, o_ref, wgb_ref, wub_ref, wdb_ref):
    i = pl.program_id(0)

    @pl.when(jnp.logical_or(i == 0, te_ref[i] != te_ref[jnp.maximum(i - 1, 0)]))
    def _():
        wgb_ref[...] = wg_ref[...].astype(BF16)
        wub_ref[...] = wu_ref[...].astype(BF16)
        wdb_ref[...] = wd_ref[...].astype(BF16)

    @pl.when(tv_ref[i] > 0)
    def _():
        half = x_ref.shape[1]
        lo, hi = _unpack_bf16_pair(x_ref[...])
        xl = lo.astype(BF16)
        xh = hi.astype(BF16)
        g = _dot_nt(xl, wgb_ref[:, :half]) + _dot_nt(xh, wgb_ref[:, half:])
        u = _dot_nt(xl, wub_ref[:, :half]) + _dot_nt(xh, wub_ref[:, half:])
        hmid = (_silu(g) * u).astype(BF16)
        y = _dot(hmid, wdb_ref[...])
        o_ref[...] = _pack_bf16_pair(y[:, :half], y[:, half:])


def _experts(xs, tile_expert, tile_valid, w_gate, w_up, w_down, layer, tm):
    n_rows, half = xs.shape
    _, n_e, ff, d = w_gate.shape
    n_tiles = n_rows // tm
    wmap = lambda i, te, tv: (layer, te[i], 0, 0)
    wspec = pl.BlockSpec((None, None, ff, d), wmap)
    return pl.pallas_call(
        _experts_kernel,
        grid_spec=pltpu.PrefetchScalarGridSpec(
            num_scalar_prefetch=2,
            grid=(n_tiles,),
            in_specs=[pl.BlockSpec((tm, half), lambda i, te, tv: (i, 0)), wspec, wspec, wspec],
            out_specs=pl.BlockSpec((tm, half), lambda i, te, tv: (i, 0)),
            scratch_shapes=[pltpu.VMEM((ff, d), BF16), pltpu.VMEM((ff, d), BF16),
                            pltpu.VMEM((ff, d), BF16)],
        ),
        out_shape=jax.ShapeDtypeStruct((n_rows, half), U32),
        compiler_params=_cparams(("arbitrary",), VMEM_LIMIT_BIG),
        name="moe_experts",
    )(tile_expert, tile_valid, xs, w_gate, w_up, w_down)


def _combine_kernel(pos_ref, wt_ref, h_ref, wsg_ref, wsu_ref, wsd_ref, g_ref, b_ref, y_hbm,
                    of_ref, ob_ref, ybuf, sem, wsgb_ref, wsub_ref, wsdb_ref, *, alpha):
    k_top, bt = pos_ref.shape

    def start_body(t, carry):
        for k in range(k_top):
            pltpu.make_async_copy(y_hbm.at[pl.ds(pos_ref[k, t], 1)], ybuf.at[k, pl.ds(t, 1)],
                                  sem).start(priority=k % 2)
        return carry

    lax.fori_loop(0, bt, start_body, 0, unroll=DMA_UNROLL)

    @pl.when(pl.program_id(0) == 0)
    def _():
        wsgb_ref[...] = wsg_ref[...].astype(BF16)
        wsub_ref[...] = wsu_ref[...].astype(BF16)
        wsdb_ref[...] = wsd_ref[...].astype(BF16)

    h = h_ref[...]
    xb = h.astype(BF16)
    mid = (_silu(_dot_nt(xb, wsgb_ref[...])) * _dot_nt(xb, wsub_ref[...])).astype(BF16)
    of_ref[...] = alpha * h + _dot(mid, wsdb_ref[...])
    half = h.shape[1] // 2

    for k in range(k_top):
        pltpu.make_async_copy(y_hbm.at[pl.ds(0, bt)], ybuf.at[k], sem).wait()

    wt = wt_ref[...]
    ch = min(COMBINE_CHUNK, half)
    wb = [jnp.broadcast_to(wt[:, k:k + 1], (bt, ch)) for k in range(k_top)]
    for c in range(half // ch):
        lo_sl = slice(c * ch, (c + 1) * ch)
        hi_sl = slice(half + c * ch, half + (c + 1) * ch)
        acc_lo = of_ref[:, lo_sl]
        acc_hi = of_ref[:, hi_sl]
        for k in range(k_top):
            lo, hi = _unpack_bf16_pair(ybuf[k, :, lo_sl])
            acc_lo = acc_lo + wb[k] * lo
            acc_hi = acc_hi + wb[k] * hi
        of_ref[:, lo_sl] = acc_lo
        of_ref[:, hi_sl] = acc_hi
    y = _ln_math(of_ref[...], g_ref[...], b_ref[...])
    of_ref[...] = y
    ob_ref[...] = y.astype(BF16)


def _combine(pos, wt, h, ws_gate, ws_up, ws_down, g, b, y, layer, alpha):
    k_top, t = pos.shape
    d = h.shape[1]
    ff = ws_gate.shape[1]
    bt = min(COMBINE_TT, t)
    row = lambda i: (i, 0)
    const = lambda i: (0, 0)
    wmap = lambda i: (layer, 0, 0)
    return pl.pallas_call(
        functools.partial(_combine_kernel, alpha=alpha),
        grid=(t // bt,),
        in_specs=[pl.BlockSpec((k_top, bt), lambda i: (0, i), memory_space=pltpu.SMEM),
                  pl.BlockSpec((bt, k_top), row),
                  pl.BlockSpec((bt, d), row),
                  pl.BlockSpec((None, ff, d), wmap), pl.BlockSpec((None, ff, d), wmap),
                  pl.BlockSpec((None, ff, d), wmap),
                  pl.BlockSpec((1, d), const), pl.BlockSpec((1, d), const),
                  pl.BlockSpec(memory_space=pl.ANY)],
        out_specs=[pl.BlockSpec((bt, d), row), pl.BlockSpec((bt, d), row)],
        out_shape=[jax.ShapeDtypeStruct((t, d), F32), jax.ShapeDtypeStruct((t, d), BF16)],
        scratch_shapes=[pltpu.VMEM((k_top, bt, d // 2), U32), pltpu.SemaphoreType.DMA(()),
                        pltpu.VMEM((ff, d), BF16), pltpu.VMEM((ff, d), BF16), pltpu.VMEM((ff, d), BF16)],
        compiler_params=_cparams(("arbitrary",), VMEM_LIMIT_BIG),
        name="moe_combine",
    )(pos, wt, h, ws_gate, ws_up, ws_down, g.reshape(1, d), b.reshape(1, d), y)


def _moe_layer(h, hb, hu, w_router, router_bias, w_gate, w_up, w_down, ws_gate, ws_up, ws_down,
               ln_g, ln_b, layer, alpha):
    t, d = h.shape
    n_e = w_router.shape[1]
    tm = MOE_TM
    n_rows = t * TOP_K + n_e * tm
    n_tiles = n_rows // tm
    idx, wt = _router(hb, w_router.T.astype(BF16), router_bias.reshape(n_e, 1))
    pos, tile_expert, tile_valid = _plan(idx, n_e, tm, n_tiles)
    xs = _dispatch(pos, hu, n_rows)
    y = _experts(xs, tile_expert.reshape(n_tiles), tile_valid.reshape(n_tiles),
                 jnp.swapaxes(w_gate, 2, 3), jnp.swapaxes(w_up, 2, 3), w_down, layer, tm)
    return _combine(pos, wt, h, jnp.swapaxes(ws_gate, 1, 2), jnp.swapaxes(ws_up, 1, 2), ws_down,
                    ln_g, ln_b, y, layer, alpha)


def kernel(x, ml_w_in, ml_b_gates, ml_norm_g, ml_w_out, rg_w_in, rg_conv_w, rg_conv_b, rg_w_ga, rg_b_ga, rg_w_gx, rg_b_gx, rg_lambda, rg_w_out, fx_w_in, fx_b_f, fx_w_out, hg_w_in, hg_gamma, hg_norm_g, hg_w_out, ln1_g, ln1_b, ln2_g, ln2_b, moe_w_router, moe_router_bias, moe_w_gate, moe_w_up, moe_w_down, moe_ws_gate, moe_ws_up, moe_ws_down):
    batch, seq, d = x.shape
    depth = ln1_g.shape[0]
    alpha = float((2 * depth) ** 0.25)
    t = batch * seq
    lbs = _lower_bounds(hg_gamma)
    h = x.reshape(t, d)
    hb = h.astype(BF16)
    for layer in range(depth):
        kind, j = layer % 4, layer // 4
        if kind == 0:
            mix = _mlstm_mixer(hb, ml_w_in[j], ml_b_gates[j], ml_norm_g[j], ml_w_out[j], batch, seq)
        elif kind == 1:
            mix = _rglru_mixer(hb, rg_w_in[j], rg_conv_w[j], rg_conv_b[j], rg_w_ga[j], rg_b_ga[j],
                               rg_w_gx[j], rg_b_gx[j], rg_lambda[j], rg_w_out[j], batch, seq)
        elif kind == 2:
            mix = _fox_mixer(hb, fx_w_in[j], fx_b_f[j], fx_w_out[j], batch, seq)
        else:
            mix = _hgrn_mixer(hb, hg_w_in[j], lbs[layer], hg_norm_g[j], hg_w_out[j], batch, seq)
        h, hb, hu = _residual_ln(h, mix, ln1_g[layer], ln1_b[layer], alpha)
        h, hb = _moe_layer(h, hb, hu, moe_w_router[layer], moe_router_bias[layer], moe_w_gate,
                           moe_w_up, moe_w_down, moe_ws_gate, moe_ws_up, moe_ws_down,
                           ln2_g[layer], ln2_b[layer], layer, alpha)
    return h.reshape(batch, seq, d)
```

```python
import functools

import jax
import jax.numpy as jnp
from jax import lax
from jax.experimental import pallas as pl
from jax.experimental.pallas import tpu as pltpu

F32 = jnp.float32
BF16 = jnp.bfloat16
I32 = jnp.int32
U32 = jnp.uint32

V7X_VMEM_BYTES = 64 * 1024 * 1024
VMEM_LIMIT_BIG = V7X_VMEM_BYTES - 8 * 1024 * 1024
VMEM_LIMIT_MID = 40 * 1024 * 1024

LN_EPS = 1e-5
RMS_EPS = 1e-6
RG_C = 8.0
ROUTED_SCALE = 2.5
N_GROUPS = 8
TOPK_GROUPS = 4
TOP_K = 8

MM_BM = 1024
MM_BN = 512
ML_CHUNK = 64
ML_BLOCK = 512
ML_HEADS_PER_STEP = 4
RG_TILE = 256
FX_TQ = 512
FX_TK = 512
FX_HEADS_PER_STEP = 4
HG_CHUNK = 256
HG_BASE = 8
MOE_TM = 512
DISPATCH_TT = 512
COMBINE_TT = 128
COMBINE_CHUNK = 256
DMA_UNROLL = 4
NEG_BIG = -1e30
LOG2_E = 1.4426950408889634


def _cparams(sem, vmem=VMEM_LIMIT_MID):
    return pltpu.CompilerParams(dimension_semantics=sem, vmem_limit_bytes=vmem)


def _sigmoid(x):
    return 0.5 * jnp.tanh(0.5 * x) + 0.5


def _silu(x):
    return x * _sigmoid(x)


def _log_sigmoid(x):
    return jnp.minimum(x, 0.0) - jnp.log1p(jnp.exp(-jnp.abs(x)))


def _softplus(x):
    return jnp.maximum(x, 0.0) + jnp.log1p(jnp.exp(-jnp.abs(x)))


def _expm1(x):
    th = jnp.tanh(0.5 * x)
    return 2.0 * th / (1.0 - th)


def _gelu_tanh(x):
    return 0.5 * x * (1.0 + jnp.tanh(0.7978845608028654 * (x + 0.044715 * x * x * x)))


def _dot(a, b):
    return jnp.dot(a, b, preferred_element_type=F32)


def _dot_nt(a, b):
    return lax.dot_general(a, b, (((1,), (1,)), ((), ())), preferred_element_type=F32)


def _dot_tn(a, b):
    return lax.dot_general(a, b, (((0,), (0,)), ((), ())), preferred_element_type=F32)


def _iota(shape, dim):
    return lax.broadcasted_iota(I32, shape, dim)


def _row_to_col(row):
    n = row.shape[1]
    eye = _iota((n, n), 0) == _iota((n, n), 1)
    return jnp.sum(jnp.where(eye, row, 0.0), axis=1, keepdims=True)


def _col_to_row(col):
    n = col.shape[0]
    eye = _iota((n, n), 0) == _iota((n, n), 1)
    return jnp.sum(jnp.where(eye, col, 0.0), axis=0, keepdims=True)


def _shift_rows(x, d, fill):
    rolled = pltpu.roll(x, d, 0)
    return jnp.where(_iota(x.shape, 0) >= d, rolled, fill)


def _pack_bf16_pair(lo, hi):
    lo_b = lax.bitcast_convert_type(lo.astype(BF16).astype(F32), U32) >> 16
    hi_b = lax.bitcast_convert_type(hi.astype(BF16).astype(F32), U32)
    return lo_b | hi_b


def _unpack_bf16_pair(w):
    lo = lax.bitcast_convert_type(w << 16, F32)
    hi = lax.bitcast_convert_type(w & jnp.uint32(0xFFFF0000), F32)
    return lo, hi


def _mm_kernel(x_ref, w_ref, o_ref, wb_ref, *, transposed):
    @pl.when(pl.program_id(1) == 0)
    def _():
        wb_ref[...] = w_ref[...].astype(BF16)

    dot = _dot_nt if transposed else _dot
    o_ref[...] = dot(x_ref[...], wb_ref[...]).astype(o_ref.dtype)


def _matmul(x, w, out_dtype, n=None, transposed=False):
    m, k = x.shape
    n_total = w.shape[0] if transposed else w.shape[1]
    n = n_total if n is None else n
    bm, bn = min(MM_BM, m), min(MM_BN, n)
    assert m % bm == 0 and n % bn == 0
    wblock = (bn, k) if transposed else (k, bn)
    wmap = (lambda j, i: (j, 0)) if transposed else (lambda j, i: (0, j))
    return pl.pallas_call(
        functools.partial(_mm_kernel, transposed=transposed),
        grid=(n // bn, m // bm),
        in_specs=[pl.BlockSpec((bm, k), lambda j, i: (i, 0)), pl.BlockSpec(wblock, wmap)],
        out_specs=pl.BlockSpec((bm, bn), lambda j, i: (i, j)),
        out_shape=jax.ShapeDtypeStruct((m, n), out_dtype),
        scratch_shapes=[pltpu.VMEM(wblock, BF16)],
        compiler_params=_cparams(("parallel", "arbitrary"), VMEM_LIMIT_BIG),
        name="matmul_nt" if transposed else "matmul",
    )(x, w)


def _gates_kernel(w_ref, x_ref, b_ref, o_ref, *, ls_from):
    z = _dot_nt(w_ref[...], x_ref[...]) + b_ref[...]
    row = _iota(z.shape, 0)
    o_ref[...] = jnp.where(row >= ls_from, _log_sigmoid(z), z)


def _gates_t(x, w_t, bias_col, ls_from, bt=1024):
    t, d = x.shape
    g = w_t.shape[0]
    bt = min(bt, t)
    return pl.pallas_call(
        functools.partial(_gates_kernel, ls_from=ls_from),
        grid=(t // bt,),
        in_specs=[pl.BlockSpec((g, d), lambda i: (0, 0)),
                  pl.BlockSpec((bt, d), lambda i: (i, 0)),
                  pl.BlockSpec((g, 1), lambda i: (0, 0))],
        out_specs=pl.BlockSpec((g, bt), lambda i: (0, i)),
        out_shape=jax.ShapeDtypeStruct((g, t), F32),
        compiler_params=_cparams(("parallel",)),
        name="gates_t",
    )(w_t, x, bias_col)


def _ln_math(x, g, b):
    mu = jnp.mean(x, axis=-1, keepdims=True)
    xc = x - mu
    var = jnp.mean(xc * xc, axis=-1, keepdims=True)
    return xc * lax.rsqrt(var + LN_EPS) * g + b


def _ln_kernel(h_ref, m_ref, g_ref, b_ref, of_ref, ob_ref, ou_ref, *, alpha):
    y = _ln_math(alpha * h_ref[...] + m_ref[...].astype(F32), g_ref[...], b_ref[...])
    of_ref[...] = y
    ob_ref[...] = y.astype(BF16)
    half = y.shape[1] // 2
    ou_ref[...] = _pack_bf16_pair(y[:, :half], y[:, half:])


def _residual_ln(h, mix, g, b, alpha, bt=128):
    t, d = h.shape
    bt = min(bt, t)
    row = lambda i: (i, 0)
    return pl.pallas_call(
        functools.partial(_ln_kernel, alpha=alpha),
        grid=(t // bt,),
        in_specs=[pl.BlockSpec((bt, d), row), pl.BlockSpec((bt, d), row),
                  pl.BlockSpec((1, d), lambda i: (0, 0)), pl.BlockSpec((1, d), lambda i: (0, 0))],
        out_specs=[pl.BlockSpec((bt, d), row), pl.BlockSpec((bt, d), row),
                   pl.BlockSpec((bt, d // 2), row)],
        out_shape=[jax.ShapeDtypeStruct((t, d), F32), jax.ShapeDtypeStruct((t, d), BF16),
                   jax.ShapeDtypeStruct((t, d // 2), U32)],
        compiler_params=_cparams(("parallel",)),
        name="residual_ln",
    )(h, mix, g.reshape(1, d), b.reshape(1, d))


def _mlstm_kernel(q_ref, k_ref, v_ref, og_ref, gi_ref, gf_ref, ng_ref, o_ref,
                  c_ref, n_ref, m_ref, *, chunk, scale, dk, dv):
    @pl.when(pl.program_id(2) == 0)
    def _():
        c_ref[...] = jnp.zeros_like(c_ref)
        n_ref[...] = jnp.zeros_like(n_ref)
        m_ref[...] = jnp.zeros_like(m_ref)

    L = chunk
    n_chunks = q_ref.shape[0] // L
    nh = q_ref.shape[1] // dk
    row = _iota((L, L), 0)
    col = _iota((L, L), 1)
    tri = row >= col

    def head_step(a, j, r0):
        q = q_ref[pl.ds(r0, L), a * dk:(a + 1) * dk]
        k = k_ref[pl.ds(r0, L), a * dk:(a + 1) * dk]
        v = v_ref[pl.ds(r0, L), a * dv:(a + 1) * dv]
        li_row = gi_ref[a, pl.ds(j, 1), :]
        lf_row = gf_ref[a, pl.ds(j, 1), :]
        lf_col = _row_to_col(lf_row)
        g_col = jnp.sum(jnp.where(tri, lf_row, 0.0), axis=1, keepdims=True)
        g_row = jnp.sum(jnp.where(row <= col, lf_col, 0.0), axis=0, keepdims=True)
        a_row = li_row - g_row
        a_col = _row_to_col(a_row)
        amax_col = jnp.max(jnp.where(tri, a_row, NEG_BIG), axis=1, keepdims=True)
        m_prev = m_ref[a]
        mm_col = jnp.maximum(amax_col, m_prev)
        e = jnp.where(tri, jnp.exp(jnp.minimum(a_row - mm_col, 0.0)), 0.0)
        w_inter = jnp.exp(m_prev - mm_col)
        qs = (q.astype(F32) * scale).astype(BF16)
        s = _dot_nt(qs, k) * e
        c_prev = c_ref[a]
        num = w_inter * _dot(qs, c_prev.astype(BF16)) + _dot(s.astype(BF16), v)
        qn = jnp.sum(qs.astype(F32) * n_ref[a], axis=1, keepdims=True)
        den = w_inter * qn + jnp.sum(s, axis=1, keepdims=True)
        m_t = g_col + mm_col
        h = num / jnp.maximum(jnp.abs(den), jnp.exp(-m_t))
        hn = h * lax.rsqrt(jnp.mean(h * h, axis=1, keepdims=True) + RMS_EPS) * ng_ref[a]
        og = og_ref[pl.ds(r0, L), a * dv:(a + 1) * dv].astype(F32)
        o_ref[pl.ds(r0, L), a * dv:(a + 1) * dv] = (hn * _sigmoid(og)).astype(o_ref.dtype)
        a_last = jnp.max(a_row, axis=1, keepdims=True)
        mm_last = jnp.maximum(a_last, m_prev)
        g_last = jnp.sum(lf_row, axis=1, keepdims=True)
        wu_col = jnp.exp(a_col - mm_last)
        decay = jnp.exp(m_prev - mm_last)
        wv = (wu_col * v.astype(F32)).astype(BF16)
        c_ref[a] = decay * c_prev + _dot_tn(k, wv)
        n_ref[a] = decay * n_ref[a] + jnp.sum(wu_col * k.astype(F32), axis=0, keepdims=True)
        m_ref[a] = g_last + mm_last

    def body(j, carry):
        r0 = pl.multiple_of(j * L, L)
        for a in range(nh):
            head_step(a, j, r0)
        return carry

    lax.fori_loop(0, n_chunks, body, 0)


def _mlstm_cell(p, gates_c, norm_g, batch, seq, heads, dk, dv):
    L = ML_CHUNK
    lb = min(ML_BLOCK, seq)
    nb = seq // lb
    cpb = lb // L
    nh = ML_HEADS_PER_STEP
    assert heads % nh == 0 and (2 * heads * dk) % (nh * dv) == 0
    hp = heads // nh
    v0 = (2 * heads * dk) // (nh * dv)
    kern = functools.partial(_mlstm_kernel, chunk=L, scale=float(dk) ** -0.5, dk=dk, dv=dv)
    return pl.pallas_call(
        kern,
        grid=(batch, hp, nb),
        in_specs=[
            pl.BlockSpec((None, lb, nh * dk), lambda b, h, c: (b, c, h)),
            pl.BlockSpec((None, lb, nh * dk), lambda b, h, c: (b, c, hp + h)),
            pl.BlockSpec((None, lb, nh * dv), lambda b, h, c: (b, c, v0 + h)),
            pl.BlockSpec((None, lb, nh * dv), lambda b, h, c: (b, c, v0 + hp + h)),
            pl.BlockSpec((nh, cpb, L), lambda b, h, c: (h, b * nb + c, 0)),
            pl.BlockSpec((nh, cpb, L), lambda b, h, c: (hp + h, b * nb + c, 0)),
            pl.BlockSpec((nh, 1, dv), lambda b, h, c: (h, 0, 0)),
        ],
        out_specs=pl.BlockSpec((None, lb, nh * dv), lambda b, h, c: (b, c, h)),
        out_shape=jax.ShapeDtypeStruct((batch, seq, heads * dv), BF16),
        scratch_shapes=[pltpu.VMEM((nh, dk, dv), F32), pltpu.VMEM((nh, 1, dk), F32),
                        pltpu.VMEM((nh, 1, 1), F32)],
        compiler_params=_cparams(("parallel", "parallel", "arbitrary")),
        name="mlstm_cell",
    )(p, p, p, p, gates_c, gates_c, norm_g.reshape(heads, 1, dv))


def _mlstm_mixer(hb, w_in, b_gates, norm_g, w_out, batch, seq):
    heads = b_gates.shape[0] // 2
    dv = norm_g.shape[1]
    dk = (w_in.shape[1] - 2 * heads - 2 * heads * dv) // (2 * heads)
    n_main = 2 * heads * dk + 2 * heads * dv
    t = batch * seq
    w_t = jnp.swapaxes(w_in, 0, 1)
    p = _matmul(hb, w_t, BF16, n=n_main, transposed=True)
    w_g_t = w_t[n_main:].astype(BF16)
    gates = _gates_t(hb, w_g_t, b_gates.reshape(-1, 1), ls_from=heads)
    gates_c = gates.reshape(2 * heads, t // ML_CHUNK, ML_CHUNK)
    cell = _mlstm_cell(p.reshape(batch, seq, n_main), gates_c, norm_g, batch, seq, heads, dk, dv)
    return _matmul(cell.reshape(t, heads * dv), w_out, F32)


def _rglru_kernel(y_ref, u_ref, cw_ref, cb_ref, wa_ref, ba_ref, wx_ref, bx_ref, lam_ref, o_ref,
                  tail_ref, hc_ref):
    c = pl.program_id(2)

    @pl.when(c == 0)
    def _():
        tail_ref[...] = jnp.zeros_like(tail_ref)
        hc_ref[...] = jnp.zeros_like(hc_ref)

    ts = u_ref.shape[0]
    u = u_ref[...].astype(F32)
    cw = cw_ref[...]
    taps = cw.shape[0]
    ext = jnp.concatenate([tail_ref[...], u], axis=0)
    uc = cw[taps - 1:taps, :] * u + cb_ref[...]
    for j in range(taps - 1):
        d = taps - 1 - j
        uc = uc + cw[j:j + 1, :] * ext[8 - d:8 - d + ts, :]
    tail_ref[...] = u[ts - 8:, :]

    ub = uc.astype(BF16)
    r = _sigmoid(_dot(ub, wa_ref[...]) + ba_ref[...])
    i = _sigmoid(_dot(ub, wx_ref[...]) + bx_ref[...])
    log_a = -RG_C * r * _softplus(-lam_ref[...])
    a = jnp.exp(log_a)
    mult = jnp.sqrt(-_expm1(2.0 * log_a))
    first = jnp.logical_and(c == 0, _iota(a.shape, 0) == 0)
    mult = jnp.where(first, 1.0, mult)
    bv = mult * i * uc
    d = 1
    while d < ts:
        a_sh = _shift_rows(a, d, 1.0)
        b_sh = _shift_rows(bv, d, 0.0)
        bv = a * b_sh + bv
        a = a * a_sh
        d *= 2
    h = bv + a * hc_ref[...]
    hc_ref[...] = h[ts - 1:ts, :]
    o_ref[...] = (h * _gelu_tanh(y_ref[...].astype(F32))).astype(o_ref.dtype)


def _rglru_cell(p, conv_w, conv_b, w_ga, b_ga, w_gx, b_gx, lam, batch, seq):
    heads, blk = b_ga.shape
    width = heads * blk
    ts = min(RG_TILE, seq)
    taps = conv_w.shape[0]
    hmap = lambda b, h, c: (h, 0, 0)
    cmap = lambda b, h, c: (0, h)
    return pl.pallas_call(
        _rglru_kernel,
        grid=(batch, heads, seq // ts),
        in_specs=[
            pl.BlockSpec((None, ts, blk), lambda b, h, c: (b, c, h)),
            pl.BlockSpec((None, ts, blk), lambda b, h, c: (b, c, heads + h)),
            pl.BlockSpec((taps, blk), cmap),
            pl.BlockSpec((1, blk), cmap),
            pl.BlockSpec((None, blk, blk), hmap),
            pl.BlockSpec((None, 1, blk), hmap),
            pl.BlockSpec((None, blk, blk), hmap),
            pl.BlockSpec((None, 1, blk), hmap),
            pl.BlockSpec((1, blk), cmap),
        ],
        out_specs=pl.BlockSpec((None, ts, blk), lambda b, h, c: (b, c, h)),
        out_shape=jax.ShapeDtypeStruct((batch, seq, width), BF16),
        scratch_shapes=[pltpu.VMEM((8, blk), F32), pltpu.VMEM((1, blk), F32)],
        compiler_params=_cparams(("parallel", "parallel", "arbitrary")),
        name="rglru_cell",
    )(p, p, conv_w, conv_b.reshape(1, width), w_ga.astype(BF16), b_ga.reshape(heads, 1, blk),
      w_gx.astype(BF16), b_gx.reshape(heads, 1, blk), lam.reshape(1, width))


def _rglru_mixer(hb, w_in, conv_w, conv_b, w_ga, b_ga, w_gx, b_gx, lam, w_out, batch, seq):
    t = batch * seq
    width = conv_b.shape[0]
    p = _matmul(hb, w_in, BF16)
    cell = _rglru_cell(p.reshape(batch, seq, 2 * width), conv_w, conv_b, w_ga, b_ga, w_gx, b_gx,
                       lam, batch, seq)
    return _matmul(cell.reshape(t, width), w_out, F32)


def _cumsum_lanes_kernel(x_ref, o_ref):
    x = x_ref[...]
    n = x.shape[1]
    lane = _iota(x.shape, 1) % 128
    d = 1
    while d < 128:
        x = x + jnp.where(lane >= d, pltpu.roll(x, d, 1), 0.0)
        d *= 2
    carry = jnp.zeros((x.shape[0], 1), F32)
    for b in range(n // 128):
        blk = x[:, b * 128:(b + 1) * 128] + carry
        o_ref[:, b * 128:(b + 1) * 128] = blk
        carry = blk[:, 127:128]


def _cumsum_lanes(x, seg):
    g, t = x.shape
    return pl.pallas_call(
        _cumsum_lanes_kernel,
        grid=(t // seg,),
        in_specs=[pl.BlockSpec((g, seg), lambda i: (0, i))],
        out_specs=pl.BlockSpec((g, seg), lambda i: (0, i)),
        out_shape=jax.ShapeDtypeStruct((g, t), F32),
        compiler_params=_cparams(("parallel",)),
        name="cumsum_lanes",
    )(x)


def _fox_kernel(q_ref, k_ref, v_ref, f_ref, o_ref, v1_ref, *, scale, hd, tk):
    qi = pl.program_id(2)
    tq = q_ref.shape[0]
    n_diag = tq // tk
    nh = q_ref.shape[1] // hd

    @pl.when(qi == 0)
    def _():
        for a in range(nh):
            v1_ref[a, :, :hd] = v_ref[:, a * hd:(a + 1) * hd]
            v1_ref[a, :, hd:] = jnp.ones((v_ref.shape[0], hd), v1_ref.dtype)

    qs = [(q_ref[:, a * hd:(a + 1) * hd].astype(F32) * scale).astype(BF16) for a in range(nh)]

    def step(j, carry, diag_off):
        c0 = pl.multiple_of(j * tk, tk)
        out = []
        for a in range(nh):
            m, acc = carry[a]
            kb = k_ref[pl.ds(c0, tk), a * hd:(a + 1) * hd]
            vb = v1_ref[a, pl.ds(c0, tk), :]
            fk = f_ref[a, :, pl.ds(c0, tk)]
            s = _dot_nt(qs[a], kb) - fk
            if diag_off is not None:
                s = jnp.where(_iota((tq, tk), 0) >= _iota((tq, tk), 1) + diag_off, s, NEG_BIG)
            m_new = jnp.maximum(m, jnp.max(s, axis=1, keepdims=True))
            p = jnp.exp((s - m_new).astype(BF16))
            out.append((m_new, jnp.exp(m - m_new) * acc + _dot(p, vb)))
        return tuple(out)

    init = tuple((jnp.full((tq, 1), NEG_BIG, F32), jnp.zeros((tq, 2 * hd), F32)) for _ in range(nh))
    final = lax.fori_loop(0, qi * n_diag, functools.partial(step, diag_off=None), init)
    for r in range(n_diag):
        final = step(qi * n_diag + r, final, r * tk)
    for a in range(nh):
        acc = final[a][1]
        o_ref[:, a * hd:(a + 1) * hd] = (acc[:, :hd] / acc[:, hd:hd + 1]).astype(o_ref.dtype)


def _fox_attention(p, f_cum, batch, seq, heads, hd):
    tq = min(FX_TQ, seq)
    tk = min(FX_TK, tq)
    assert tq % tk == 0
    nh = FX_HEADS_PER_STEP
    assert heads % nh == 0
    hp = heads // nh
    w = nh * hd
    return pl.pallas_call(
        functools.partial(_fox_kernel, scale=float(hd) ** -0.5, hd=hd, tk=tk),
        grid=(batch, hp, seq // tq),
        in_specs=[
            pl.BlockSpec((None, tq, w), lambda b, h, i: (b, i, h)),
            pl.BlockSpec((None, seq, w), lambda b, h, i: (b, 0, hp + h)),
            pl.BlockSpec((None, seq, w), lambda b, h, i: (b, 0, 2 * hp + h)),
            pl.BlockSpec((nh, 1, seq), lambda b, h, i: (h, 0, b)),
        ],
        out_specs=pl.BlockSpec((None, tq, w), lambda b, h, i: (b, i, h)),
        out_shape=jax.ShapeDtypeStruct((batch, seq, heads * hd), BF16),
        scratch_shapes=[pltpu.VMEM((nh, seq, 2 * hd), BF16)],
        compiler_params=_cparams(("parallel", "parallel", "arbitrary"), VMEM_LIMIT_BIG),
        name="fox_attention",
    )(p, p, p, f_cum)


def _fox_mixer(hb, w_in, b_f, w_out, batch, seq):
    heads = b_f.shape[0]
    d_model = w_out.shape[0]
    hd = d_model // heads
    t = batch * seq
    w_t = jnp.swapaxes(w_in, 0, 1)
    p = _matmul(hb, w_t, BF16, n=3 * d_model, transposed=True)
    w_f_t = w_t[3 * d_model:].astype(BF16)
    log_f = _gates_t(hb, w_f_t, b_f.reshape(-1, 1), ls_from=0)
    f_cum = _cumsum_lanes(log_f, seq).reshape(heads, 1, t)
    o = _fox_attention(p.reshape(batch, seq, 3 * d_model), f_cum, batch, seq, heads, hd)
    return _matmul(o.reshape(t, d_model), w_out, F32)


def _lower_bounds_kernel(g_ref, o_ref):
    g = g_ref[...]
    e = jnp.exp(g - jnp.max(g, axis=0, keepdims=True))
    p = e / jnp.sum(e, axis=0, keepdims=True)
    layer = _iota(p.shape, 0)
    rows = [jnp.sum(jnp.where(layer <= i, p, 0.0), axis=0, keepdims=True) for i in range(g.shape[0])]
    o_ref[...] = jnp.concatenate(rows, axis=0) - p[0:1, :]


def _lower_bounds(gamma):
    return pl.pallas_call(
        _lower_bounds_kernel,
        out_shape=jax.ShapeDtypeStruct(gamma.shape, F32),
        name="hgrn_lower_bounds",
    )(gamma)


def _hgrn_kernel(q_ref, f_ref, i_ref, g_ref, lb_ref, ng_ref, o_ref, s_ref):
    @pl.when(pl.program_id(2) == 0)
    def _():
        s_ref[...] = jnp.zeros_like(s_ref)

    lc, d = q_ref.shape
    q = _silu(q_ref[...].astype(F32))
    lb = lb_ref[...]
    la = jnp.log(lb)
    b_ = jnp.log1p(-lb) + _log_sigmoid(f_ref[...].astype(F32))
    log_f = jnp.maximum(la, b_) + jnp.log1p(jnp.exp(-jnp.abs(la - b_)))
    k = -_expm1(log_f)
    vb = i_ref[...]
    v = vb.astype(F32)
    c = log_f
    sh = 1
    while sh < lc:
        c = c + _shift_rows(c, sh, 0.0)
        sh *= 2
    c = c * LOG2_E
    tot = c[lc - 1:lc, :]

    s_prev = s_ref[...]
    o = _dot((q * jnp.exp2(c)).astype(BF16), s_prev.astype(BF16))
    row = _iota((lc, lc), 0)
    col = _iota((lc, lc), 1)
    x = jnp.where(row > col, row ^ col, 0)
    amat = jnp.zeros((lc, lc), F32)
    b = HG_BASE
    while b < lc:
        rho = jnp.broadcast_to(c.reshape(lc // (2 * b), 2 * b, d)[:, b - 1:b, :],
                               (lc // (2 * b), 2 * b, d)).reshape(lc, d)
        qq = (q * jnp.exp2(c - rho)).astype(BF16)
        kk = (k * jnp.exp2(rho - c)).astype(BF16)
        amat = jnp.where(jnp.logical_and(x >= b, x < 2 * b), _dot_nt(qq, kk), amat)
        b *= 2
    o = o + _dot(amat.astype(BF16), vb)
    nb = lc // HG_BASE
    q3 = q.reshape(nb, HG_BASE, d)
    k3 = k.reshape(nb, HG_BASE, d)
    c3 = c.reshape(nb, HG_BASE, d)
    v3 = v.reshape(nb, HG_BASE, d)
    tpos = _iota((nb, HG_BASE, d), 1)
    od = jnp.zeros((nb, HG_BASE, d), F32)
    for s in range(HG_BASE):
        term = q3 * k3[:, s:s + 1, :] * jnp.exp2(c3 - c3[:, s:s + 1, :])
        a_s = jnp.sum(jnp.where(tpos >= s, term, 0.0), axis=2, keepdims=True)
        od = od + a_s * v3[:, s:s + 1, :]
    o = o + od.reshape(lc, d)
    kd = (k * jnp.exp2(tot - c)).astype(BF16)
    s_ref[...] = _row_to_col(jnp.exp2(tot)) * s_prev + _dot_tn(kd, vb)
    on = o * lax.rsqrt(jnp.mean(o * o, axis=1, keepdims=True) + RMS_EPS) * ng_ref[...]
    o_ref[...] = (on * _silu(g_ref[...].astype(F32))).astype(o_ref.dtype)


def _hgrn_cell(p, lb, norm_g, batch, seq):
    heads, d = norm_g.shape
    lc = min(HG_CHUNK, seq)
    return pl.pallas_call(
        _hgrn_kernel,
        grid=(batch, heads, seq // lc),
        in_specs=[
            pl.BlockSpec((None, lc, d), lambda b, h, c: (b, c, h)),
            pl.BlockSpec((None, lc, d), lambda b, h, c: (b, c, heads + h)),
            pl.BlockSpec((None, lc, d), lambda b, h, c: (b, c, 2 * heads + h)),
            pl.BlockSpec((None, lc, d), lambda b, h, c: (b, c, 3 * heads + h)),
            pl.BlockSpec((1, d), lambda b, h, c: (0, h)),
            pl.BlockSpec((None, 1, d), lambda b, h, c: (h, 0, 0)),
        ],
        out_specs=pl.BlockSpec((None, lc, d), lambda b, h, c: (b, c, h)),
        out_shape=jax.ShapeDtypeStruct((batch, seq, heads * d), BF16),
        scratch_shapes=[pltpu.VMEM((d, d), F32)],
        compiler_params=_cparams(("parallel", "parallel", "arbitrary")),
        name="hgrn_cell",
    )(p, p, p, p, lb.reshape(1, heads * d), norm_g.reshape(heads, 1, d))


def _hgrn_mixer(hb, w_in, lb, norm_g, w_out, batch, seq):
    t = batch * seq
    d_model = w_out.shape[0]
    p = _matmul(hb, w_in, BF16)
    cell = _hgrn_cell(p.reshape(batch, seq, 4 * d_model), lb, norm_g, batch, seq)
    return _matmul(cell.reshape(t, d_model), w_out, F32)


def _router_kernel(w_ref, x_ref, b_ref, idx_ref, wt_ref):
    logits = _dot_nt(w_ref[...], x_ref[...])
    n_e, bt = logits.shape
    gs = n_e // N_GROUPS
    scores = _sigmoid(logits)
    sel = scores + b_ref[...]
    sel3 = sel.reshape(N_GROUPS, gs, bt)
    pos3 = _iota(sel3.shape, 1)
    m1 = jnp.max(sel3, axis=1, keepdims=True)
    i1 = jnp.min(jnp.where(sel3 == m1, pos3, gs), axis=1, keepdims=True)
    m2 = jnp.max(jnp.where(pos3 == i1, -jnp.inf, sel3), axis=1, keepdims=True)
    gscore = (m1 + m2).reshape(N_GROUPS, bt)
    gid = _iota(gscore.shape, 0)
    gmask = jnp.zeros(gscore.shape, F32)
    for _ in range(TOPK_GROUPS):
        gm = jnp.max(gscore, axis=0, keepdims=True)
        gi = jnp.min(jnp.where(gscore == gm, gid, N_GROUPS), axis=0, keepdims=True)
        hit = gid == gi
        gmask = jnp.where(hit, 1.0, gmask)
        gscore = jnp.where(hit, -jnp.inf, gscore)
    emask = jnp.broadcast_to(gmask.reshape(N_GROUPS, 1, bt), (N_GROUPS, gs, bt)).reshape(n_e, bt)
    cand = jnp.where(emask > 0.5, sel, -jnp.inf)
    eid = _iota(cand.shape, 0)
    idx_rows, w_rows = [], []
    for _ in range(TOP_K):
        cm = jnp.max(cand, axis=0, keepdims=True)
        ci = jnp.min(jnp.where(cand == cm, eid, n_e), axis=0, keepdims=True)
        hit = eid == ci
        idx_rows.append(ci)
        w_rows.append(jnp.sum(jnp.where(hit, scores, 0.0), axis=0, keepdims=True))
        cand = jnp.where(hit, -jnp.inf, cand)
    w = jnp.concatenate(w_rows, axis=0)
    w = w / jnp.sum(w, axis=0, keepdims=True) * ROUTED_SCALE
    idx_ref[...] = jnp.concatenate(idx_rows, axis=0)
    eye = (_iota((bt, bt), 0) == _iota((bt, bt), 1)).astype(BF16)
    w1 = w.astype(BF16)
    r1 = w - w1.astype(F32)
    w2 = r1.astype(BF16)
    w3 = (r1 - w2.astype(F32)).astype(BF16)
    wt_ref[...] = _dot_nt(eye, w1) + _dot_nt(eye, w2) + _dot_nt(eye, w3)


def _router(hb, w_router_t, bias_col, bt=512):
    t, d = hb.shape
    n_e = w_router_t.shape[0]
    bt = min(bt, t)
    return pl.pallas_call(
        _router_kernel,
        grid=(t // bt,),
        in_specs=[pl.BlockSpec((n_e, d), lambda i: (0, 0)),
                  pl.BlockSpec((bt, d), lambda i: (i, 0)),
                  pl.BlockSpec((n_e, 1), lambda i: (0, 0))],
        out_specs=[pl.BlockSpec((TOP_K, bt), lambda i: (0, i)),
                   pl.BlockSpec((bt, TOP_K), lambda i: (i, 0))],
        out_shape=[jax.ShapeDtypeStruct((TOP_K, t), I32), jax.ShapeDtypeStruct((t, TOP_K), F32)],
        compiler_params=_cparams(("parallel",)),
        name="moe_router",
    )(w_router_t, hb, bias_col)


def _plan_kernel(idx_ref, pos_ref, te_ref, tv_ref, *, n_e, tm, bt):
    k_top, t = idx_ref.shape
    n_steps = t // bt
    eid = _iota((n_e, bt), 0)
    upper = (_iota((bt, bt), 0) <= _iota((bt, bt), 1)).astype(BF16)

    def onehots(j):
        idx = idx_ref[:, pl.ds(pl.multiple_of(j * bt, bt), bt)]
        return [eid == idx[k:k + 1, :] for k in range(k_top)]

    def rank_body(j, carry):
        hits = onehots(j)
        m = jnp.zeros((n_e, bt), F32)
        for hmask in hits:
            m = m + hmask.astype(F32)
        cum = _dot(m.astype(BF16), upper)
        rank = carry + cum - m
        rows = [jnp.sum(jnp.where(hmask, rank, 0.0), axis=0, keepdims=True) for hmask in hits]
        pos_ref[:, pl.ds(pl.multiple_of(j * bt, bt), bt)] = jnp.concatenate(rows, axis=0).astype(I32)
        return carry + cum[:, bt - 1:bt]

    counts = lax.fori_loop(0, n_steps, rank_body, jnp.zeros((n_e, 1), F32))
    shift = tm.bit_length() - 1
    padded = (((counts.astype(I32) + (tm - 1)) >> shift) << shift).astype(F32)
    padded_row = _col_to_row(padded)
    below = _iota((n_e, n_e), 1) < _iota((n_e, n_e), 0)
    off = jnp.sum(jnp.where(below, padded_row, 0.0), axis=1, keepdims=True)

    def off_body(j, carry):
        hits = onehots(j)
        rows = [jnp.sum(jnp.where(hmask, off, 0.0), axis=0, keepdims=True) for hmask in hits]
        sl = pl.ds(pl.multiple_of(j * bt, bt), bt)
        pos_ref[:, sl] = pos_ref[:, sl] + jnp.concatenate(rows, axis=0).astype(I32)
        return carry

    lax.fori_loop(0, n_steps, off_body, 0)
    n_tiles = te_ref.shape[1]
    start = (_iota((n_e, n_tiles), 1) * tm).astype(F32)
    inside = jnp.logical_and(start >= off, start < off + padded)
    ecol = _iota((n_e, n_tiles), 0).astype(F32)
    te = jnp.sum(jnp.where(inside, ecol, 0.0), axis=0, keepdims=True)
    valid = jnp.sum(inside.astype(F32), axis=0, keepdims=True)
    last_e = jnp.max(jnp.where(padded > 0.0, ecol[:, 0:1], 0.0), axis=0, keepdims=True)
    te_ref[...] = jnp.where(valid > 0.0, te, last_e).astype(I32)
    tv_ref[...] = valid.astype(I32)


def _plan(idx, n_e, tm, n_tiles):
    k_top, t = idx.shape
    bt = min(512, t)
    return pl.pallas_call(
        functools.partial(_plan_kernel, n_e=n_e, tm=tm, bt=bt),
        out_shape=[jax.ShapeDtypeStruct((k_top, t), I32),
                   jax.ShapeDtypeStruct((1, n_tiles), I32),
                   jax.ShapeDtypeStruct((1, n_tiles), I32)],
        compiler_params=pltpu.CompilerParams(vmem_limit_bytes=VMEM_LIMIT_MID),
        name="moe_plan",
    )(idx)


def _dispatch_kernel(pos_ref, x_ref, xs_hbm, sem):
    k_top, bt = pos_ref.shape

    def start_body(t, carry):
        for k in range(k_top):
            pltpu.make_async_copy(x_ref.at[pl.ds(t, 1)], xs_hbm.at[pl.ds(pos_ref[k, t], 1)],
                                  sem).start(priority=k % 2)
        return carry

    lax.fori_loop(0, bt, start_body, 0, unroll=DMA_UNROLL)
    for k in range(k_top):
        pltpu.make_async_copy(x_ref, xs_hbm.at[pl.ds(0, bt)], sem).wait()


def _dispatch(pos, x_packed, n_rows):
    k_top, t = pos.shape
    half = x_packed.shape[1]
    bt = min(DISPATCH_TT, t)
    return pl.pallas_call(
        _dispatch_kernel,
        grid=(t // bt,),
        in_specs=[pl.BlockSpec((k_top, bt), lambda i: (0, i), memory_space=pltpu.SMEM),
                  pl.BlockSpec((bt, half), lambda i: (i, 0))],
        out_specs=pl.BlockSpec(memory_space=pl.ANY),
        out_shape=jax.ShapeDtypeStruct((n_rows, half), U32),
        scratch_shapes=[pltpu.SemaphoreType.DMA(())],
        compiler_params=_cparams(("arbitrary",)),
        name="moe_dispatch",
    )(pos, x_packed)


def _experts_kernel(te_ref, tv_ref, x_ref, wg_ref, wu_ref, wd_ref, o_ref, wgb_ref, wub_ref, wdb_ref):
    i = pl.program_id(0)

    @pl.when(jnp.logical_or(i == 0, te_ref[i] != te_ref[jnp.maximum(i - 1, 0)]))
    def _():
        wgb_ref[...] = wg_ref[...].astype(BF16)
        wub_ref[...] = wu_ref[...].astype(BF16)
        wdb_ref[...] = wd_ref[...].astype(BF16)

    @pl.when(tv_ref[i] > 0)
    def _():
        half = x_ref.shape[1]
        lo, hi = _unpack_bf16_pair(x_ref[...])
        xl = lo.astype(BF16)
        xh = hi.astype(BF16)
        g = _dot_nt(xl, wgb_ref[:, :half]) + _dot_nt(xh, wgb_ref[:, half:])
        u = _dot_nt(xl, wub_ref[:, :half]) + _dot_nt(xh, wub_ref[:, half:])
        hmid = (_silu(g) * u).astype(BF16)
        y = _dot(hmid, wdb_ref[...])
        o_ref[...] = _pack_bf16_pair(y[:, :half], y[:, half:])


def _experts(xs, tile_expert, tile_valid, w_gate, w_up, w_down, layer, tm):
    n_rows, half = xs.shape
    _, n_e, ff, d = w_gate.shape
    n_tiles = n_rows // tm
    wmap = lambda i, te, tv: (layer, te[i], 0, 0)
    wspec = pl.BlockSpec((None, None, ff, d), wmap)
    return pl.pallas_call(
        _experts_kernel,
        grid_spec=pltpu.PrefetchScalarGridSpec(
            num_scalar_prefetch=2,
            grid=(n_tiles,),
            in_specs=[pl.BlockSpec((tm, half), lambda i, te, tv: (i, 0)), wspec, wspec, wspec],
            out_specs=pl.BlockSpec((tm, half), lambda i, te, tv: (i, 0)),
            scratch_shapes=[pltpu.VMEM((ff, d), BF16), pltpu.VMEM((ff, d), BF16),
                            pltpu.VMEM((ff, d), BF16)],
        ),
        out_shape=jax.ShapeDtypeStruct((n_rows, half), U32),
        compiler_params=_cparams(("arbitrary",), VMEM_LIMIT_BIG),
        name="moe_experts",
    )(tile_expert, tile_valid, xs, w_gate, w_up, w_down)


def _combine_kernel(pos_ref, wt_ref, h_ref, wsg_ref, wsu_ref, wsd_ref, g_ref, b_ref, y_hbm,
                    of_ref, ob_ref, ybuf, sem, wsgb_ref, wsub_ref, wsdb_ref, *, alpha):
    k_top, bt = pos_ref.shape

    def start_body(t, carry):
        for k in range(k_top):
            pltpu.make_async_copy(y_hbm.at[pl.ds(pos_ref[k, t], 1)], ybuf.at[k, pl.ds(t, 1)], sem).start()
        return carry

    lax.fori_loop(0, bt, start_body, 0, unroll=DMA_UNROLL)

    @pl.when(pl.program_id(0) == 0)
    def _():
        wsgb_ref[...] = wsg_ref[...].astype(BF16)
        wsub_ref[...] = wsu_ref[...].astype(BF16)
        wsdb_ref[...] = wsd_ref[...].astype(BF16)

    h = h_ref[...]
    xb = h.astype(BF16)
    mid = (_silu(_dot_nt(xb, wsgb_ref[...])) * _dot_nt(xb, wsub_ref[...])).astype(BF16)
    of_ref[...] = alpha * h + _dot(mid, wsdb_ref[...])
    half = h.shape[1] // 2

    for k in range(k_top):
        pltpu.make_async_copy(y_hbm.at[pl.ds(0, bt)], ybuf.at[k], sem).wait()

    wt = wt_ref[...]
    ch = min(COMBINE_CHUNK, half)
    wb = [jnp.broadcast_to(wt[:, k:k + 1], (bt, ch)) for k in range(k_top)]
    for c in range(half // ch):
        lo_sl = slice(c * ch, (c + 1) * ch)
        hi_sl = slice(half + c * ch, half + (c + 1) * ch)
        acc_lo = of_ref[:, lo_sl]
        acc_hi = of_ref[:, hi_sl]
        for k in range(k_top):
            lo, hi = _unpack_bf16_pair(ybuf[k, :, lo_sl])
            acc_lo = acc_lo + wb[k] * lo
            acc_hi = acc_hi + wb[k] * hi
        of_ref[:, lo_sl] = acc_lo
        of_ref[:, hi_sl] = acc_hi
    y = _ln_math(of_ref[...], g_ref[...], b_ref[...])
    of_ref[...] = y
    ob_ref[...] = y.astype(BF16)


def _combine(pos, wt, h, ws_gate, ws_up, ws_down, g, b, y, layer, alpha):
    k_top, t = pos.shape
    d = h.shape[1]
    ff = ws_gate.shape[1]
    bt = min(COMBINE_TT, t)
    row = lambda i: (i, 0)
    const = lambda i: (0, 0)
    wmap = lambda i: (layer, 0, 0)
    return pl.pallas_call(
        functools.partial(_combine_kernel, alpha=alpha),
        grid=(t // bt,),
        in_specs=[pl.BlockSpec((k_top, bt), lambda i: (0, i), memory_space=pltpu.SMEM),
                  pl.BlockSpec((bt, k_top), row),
                  pl.BlockSpec((bt, d), row),
                  pl.BlockSpec((None, ff, d), wmap), pl.BlockSpec((None, ff, d), wmap),
                  pl.BlockSpec((None, ff, d), wmap),
                  pl.BlockSpec((1, d), const), pl.BlockSpec((1, d), const),
                  pl.BlockSpec(memory_space=pl.ANY)],
        out_specs=[pl.BlockSpec((bt, d), row), pl.BlockSpec((bt, d), row)],
        out_shape=[jax.ShapeDtypeStruct((t, d), F32), jax.ShapeDtypeStruct((t, d), BF16)],
        scratch_shapes=[pltpu.VMEM((k_top, bt, d // 2), U32), pltpu.SemaphoreType.DMA(()),
                        pltpu.VMEM((ff, d), BF16), pltpu.VMEM((ff, d), BF16), pltpu.VMEM((ff, d), BF16)],
        compiler_params=_cparams(("arbitrary",), VMEM_LIMIT_BIG),
        name="moe_combine",
    )(pos, wt, h, ws_gate, ws_up, ws_down, g.reshape(1, d), b.reshape(1, d), y)


def _moe_layer(h, hb, hu, w_router, router_bias, w_gate, w_up, w_down, ws_gate, ws_up, ws_down,
               ln_g, ln_b, layer, alpha):
    t, d = h.shape
    n_e = w_router.shape[1]
    tm = MOE_TM
    n_rows = t * TOP_K + n_e * tm
    n_tiles = n_rows // tm
    idx, wt = _router(hb, w_router.T.astype(BF16), router_bias.reshape(n_e, 1))
    pos, tile_expert, tile_valid = _plan(idx, n_e, tm, n_tiles)
    xs = _dispatch(pos, hu, n_rows)
    y = _experts(xs, tile_expert.reshape(n_tiles), tile_valid.reshape(n_tiles),
                 jnp.swapaxes(w_gate, 2, 3), jnp.swapaxes(w_up, 2, 3), w_down, layer, tm)
    return _combine(pos, wt, h, jnp.swapaxes(ws_gate, 1, 2), jnp.swapaxes(ws_up, 1, 2), ws_down,
                    ln_g, ln_b, y, layer, alpha)


def kernel(x, ml_w_in, ml_b_gates, ml_norm_g, ml_w_out, rg_w_in, rg_conv_w, rg_conv_b, rg_w_ga, rg_b_ga, rg_w_gx, rg_b_gx, rg_lambda, rg_w_out, fx_w_in, fx_b_f, fx_w_out, hg_w_in, hg_gamma, hg_norm_g, hg_w_out, ln1_g, ln1_b, ln2_g, ln2_b, moe_w_router, moe_router_bias, moe_w_gate, moe_w_up, moe_w_down, moe_ws_gate, moe_ws_up, moe_ws_down):
    batch, seq, d = x.shape
    depth = ln1_g.shape[0]
    alpha = float((2 * depth) ** 0.25)
    t = batch * seq
    lbs = _lower_bounds(hg_gamma)
    h = x.reshape(t, d)
    hb = h.astype(BF16)
    for layer in range(depth):
        kind, j = layer % 4, layer // 4
        if kind == 0:
            mix = _mlstm_mixer(hb, ml_w_in[j], ml_b_gates[j], ml_norm_g[j], ml_w_out[j], batch, seq)
        elif kind == 1:
            mix = _rglru_mixer(hb, rg_w_in[j], rg_conv_w[j], rg_conv_b[j], rg_w_ga[j], rg_b_ga[j],
                               rg_w_gx[j], rg_b_gx[j], rg_lambda[j], rg_w_out[j], batch, seq)
        elif kind == 2:
            mix = _fox_mixer(hb, fx_w_in[j], fx_b_f[j], fx_w_out[j], batch, seq)
        else:
            mix = _hgrn_mixer(hb, hg_w_in[j], lbs[layer], hg_norm_g[j], hg_w_out[j], batch, seq)
        h, hb, hu = _residual_ln(h, mix, ln1_g[layer], ln1_b[layer], alpha)
        h, hb = _moe_layer(h, hb, hu, moe_w_router[layer], moe_router_bias[layer], moe_w_gate,
                           moe_w_up, moe_w_down, moe_ws_gate, moe_ws_up, moe_ws_down,
                           ln2_g[layer], ln2_b[layer], layer, alpha)
    return h.reshape(batch, seq, d)
```

```python
import functools

import jax
import jax.numpy as jnp
from jax import lax
from jax.experimental import pallas as pl
from jax.experimental.pallas import tpu as pltpu

F32 = jnp.float32
BF16 = jnp.bfloat16
I32 = jnp.int32
U32 = jnp.uint32

V7X_VMEM_BYTES = 64 * 1024 * 1024
VMEM_LIMIT_BIG = V7X_VMEM_BYTES - 8 * 1024 * 1024
VMEM_LIMIT_MID = 40 * 1024 * 1024

LN_EPS = 1e-5
RMS_EPS = 1e-6
RG_C = 8.0
ROUTED_SCALE = 2.5
N_GROUPS = 8
TOPK_GROUPS = 4
TOP_K = 8

MM_BM = 1024
MM_BN = 512
ML_CHUNK = 64
ML_BLOCK = 512
ML_HEADS_PER_STEP = 4
RG_TILE = 256
FX_TQ = 512
FX_TK = 512
FX_HEADS_PER_STEP = 4
HG_CHUNK = 256
HG_BASE = 8
MOE_TM = 512
DISPATCH_TT = 512
COMBINE_TT = 128
COMBINE_CHUNK = 256
DMA_UNROLL = 4
NEG_BIG = -1e30
LOG2_E = 1.4426950408889634


def _cparams(sem, vmem=VMEM_LIMIT_MID):
    return pltpu.CompilerParams(dimension_semantics=sem, vmem_limit_bytes=vmem)


def _sigmoid(x):
    return 0.5 * jnp.tanh(0.5 * x) + 0.5


def _silu(x):
    return x * _sigmoid(x)


def _log_sigmoid(x):
    return jnp.minimum(x, 0.0) - jnp.log1p(jnp.exp(-jnp.abs(x)))


def _softplus(x):
    return jnp.maximum(x, 0.0) + jnp.log1p(jnp.exp(-jnp.abs(x)))


def _expm1(x):
    th = jnp.tanh(0.5 * x)
    return 2.0 * th / (1.0 - th)


def _gelu_tanh(x):
    return 0.5 * x * (1.0 + jnp.tanh(0.7978845608028654 * (x + 0.044715 * x * x * x)))


def _dot(a, b):
    return jnp.dot(a, b, preferred_element_type=F32)


def _dot_nt(a, b):
    return lax.dot_general(a, b, (((1,), (1,)), ((), ())), preferred_element_type=F32)


def _dot_tn(a, b):
    return lax.dot_general(a, b, (((0,), (0,)), ((), ())), preferred_element_type=F32)


def _iota(shape, dim):
    return lax.broadcasted_iota(I32, shape, dim)


def _row_to_col(row):
    n = row.shape[1]
    eye = _iota((n, n), 0) == _iota((n, n), 1)
    return jnp.sum(jnp.where(eye, row, 0.0), axis=1, keepdims=True)


def _col_to_row(col):
    n = col.shape[0]
    eye = _iota((n, n), 0) == _iota((n, n), 1)
    return jnp.sum(jnp.where(eye, col, 0.0), axis=0, keepdims=True)


def _shift_rows(x, d, fill):
    rolled = pltpu.roll(x, d, 0)
    return jnp.where(_iota(x.shape, 0) >= d, rolled, fill)


def _pack_bf16_pair(lo, hi):
    lo_b = lax.bitcast_convert_type(lo.astype(BF16).astype(F32), U32) >> 16
    hi_b = lax.bitcast_convert_type(hi.astype(BF16).astype(F32), U32)
    return lo_b | hi_b


def _unpack_bf16_pair(w):
    lo = lax.bitcast_convert_type(w << 16, F32)
    hi = lax.bitcast_convert_type(w & jnp.uint32(0xFFFF0000), F32)
    return lo, hi


def _mm_kernel(x_ref, w_ref, o_ref, wb_ref, *, transposed):
    @pl.when(pl.program_id(1) == 0)
    def _():
        wb_ref[...] = w_ref[...].astype(BF16)

    dot = _dot_nt if transposed else _dot
    o_ref[...] = dot(x_ref[...], wb_ref[...]).astype(o_ref.dtype)


def _matmul(x, w, out_dtype, n=None, transposed=False):
    m, k = x.shape
    n_total = w.shape[0] if transposed else w.shape[1]
    n = n_total if n is None else n
    bm, bn = min(MM_BM, m), min(MM_BN, n)
    assert m % bm == 0 and n % bn == 0
    wblock = (bn, k) if transposed else (k, bn)
    wmap = (lambda j, i: (j, 0)) if transposed else (lambda j, i: (0, j))
    return pl.pallas_call(
        functools.partial(_mm_kernel, transposed=transposed),
        grid=(n // bn, m // bm),
        in_specs=[pl.BlockSpec((bm, k), lambda j, i: (i, 0)), pl.BlockSpec(wblock, wmap)],
        out_specs=pl.BlockSpec((bm, bn), lambda j, i: (i, j)),
        out_shape=jax.ShapeDtypeStruct((m, n), out_dtype),
        scratch_shapes=[pltpu.VMEM(wblock, BF16)],
        compiler_params=_cparams(("parallel", "arbitrary"), VMEM_LIMIT_BIG),
        name="matmul_nt" if transposed else "matmul",
    )(x, w)


def _gates_kernel(w_ref, x_ref, b_ref, o_ref, *, ls_from):
    z = _dot_nt(w_ref[...], x_ref[...]) + b_ref[...]
    row = _iota(z.shape, 0)
    o_ref[...] = jnp.where(row >= ls_from, _log_sigmoid(z), z)


def _gates_t(x, w_t, bias_col, ls_from, bt=1024):
    t, d = x.shape
    g = w_t.shape[0]
    bt = min(bt, t)
    return pl.pallas_call(
        functools.partial(_gates_kernel, ls_from=ls_from),
        grid=(t // bt,),
        in_specs=[pl.BlockSpec((g, d), lambda i: (0, 0)),
                  pl.BlockSpec((bt, d), lambda i: (i, 0)),
                  pl.BlockSpec((g, 1), lambda i: (0, 0))],
        out_specs=pl.BlockSpec((g, bt), lambda i: (0, i)),
        out_shape=jax.ShapeDtypeStruct((g, t), F32),
        compiler_params=_cparams(("parallel",)),
        name="gates_t",
    )(w_t, x, bias_col)


def _ln_math(x, g, b):
    mu = jnp.mean(x, axis=-1, keepdims=True)
    xc = x - mu
    var = jnp.mean(xc * xc, axis=-1, keepdims=True)
    return xc * lax.rsqrt(var + LN_EPS) * g + b


def _ln_kernel(h_ref, m_ref, g_ref, b_ref, of_ref, ob_ref, ou_ref, *, alpha):
    y = _ln_math(alpha * h_ref[...] + m_ref[...].astype(F32), g_ref[...], b_ref[...])
    of_ref[...] = y
    ob_ref[...] = y.astype(BF16)
    half = y.shape[1] // 2
    ou_ref[...] = _pack_bf16_pair(y[:, :half], y[:, half:])


def _residual_ln(h, mix, g, b, alpha, bt=128):
    t, d = h.shape
    bt = min(bt, t)
    row = lambda i: (i, 0)
    return pl.pallas_call(
        functools.partial(_ln_kernel, alpha=alpha),
        grid=(t // bt,),
        in_specs=[pl.BlockSpec((bt, d), row), pl.BlockSpec((bt, d), row),
                  pl.BlockSpec((1, d), lambda i: (0, 0)), pl.BlockSpec((1, d), lambda i: (0, 0))],
        out_specs=[pl.BlockSpec((bt, d), row), pl.BlockSpec((bt, d), row),
                   pl.BlockSpec((bt, d // 2), row)],
        out_shape=[jax.ShapeDtypeStruct((t, d), F32), jax.ShapeDtypeStruct((t, d), BF16),
                   jax.ShapeDtypeStruct((t, d // 2), U32)],
        compiler_params=_cparams(("parallel",)),
        name="residual_ln",
    )(h, mix, g.reshape(1, d), b.reshape(1, d))


def _mlstm_kernel(q_ref, k_ref, v_ref, og_ref, gi_ref, gf_ref, ng_ref, o_ref,
                  c_ref, n_ref, m_ref, *, chunk, scale, dk, dv):
    @pl.when(pl.program_id(2) == 0)
    def _():
        c_ref[...] = jnp.zeros_like(c_ref)
        n_ref[...] = jnp.zeros_like(n_ref)
        m_ref[...] = jnp.zeros_like(m_ref)

    L = chunk
    n_chunks = q_ref.shape[0] // L
    nh = q_ref.shape[1] // dk
    row = _iota((L, L), 0)
    col = _iota((L, L), 1)
    tri = row >= col

    def head_step(a, j, r0):
        q = q_ref[pl.ds(r0, L), a * dk:(a + 1) * dk]
        k = k_ref[pl.ds(r0, L), a * dk:(a + 1) * dk]
        v = v_ref[pl.ds(r0, L), a * dv:(a + 1) * dv]
        li_row = gi_ref[a, pl.ds(j, 1), :]
        lf_row = gf_ref[a, pl.ds(j, 1), :]
        lf_col = _row_to_col(lf_row)
        g_col = jnp.sum(jnp.where(tri, lf_row, 0.0), axis=1, keepdims=True)
        g_row = jnp.sum(jnp.where(row <= col, lf_col, 0.0), axis=0, keepdims=True)
        a_row = li_row - g_row
        a_col = _row_to_col(a_row)
        amax_col = jnp.max(jnp.where(tri, a_row, NEG_BIG), axis=1, keepdims=True)
        m_prev = m_ref[a]
        mm_col = jnp.maximum(amax_col, m_prev)
        e = jnp.where(tri, jnp.exp(jnp.minimum(a_row - mm_col, 0.0)), 0.0)
        w_inter = jnp.exp(m_prev - mm_col)
        qs = (q.astype(F32) * scale).astype(BF16)
        s = _dot_nt(qs, k) * e
        c_prev = c_ref[a]
        num = w_inter * _dot(qs, c_prev.astype(BF16)) + _dot(s.astype(BF16), v)
        qn = jnp.sum(qs.astype(F32) * n_ref[a], axis=1, keepdims=True)
        den = w_inter * qn + jnp.sum(s, axis=1, keepdims=True)
        m_t = g_col + mm_col
        h = num / jnp.maximum(jnp.abs(den), jnp.exp(-m_t))
        hn = h * lax.rsqrt(jnp.mean(h * h, axis=1, keepdims=True) + RMS_EPS) * ng_ref[a]
        og = og_ref[pl.ds(r0, L), a * dv:(a + 1) * dv].astype(F32)
        o_ref[pl.ds(r0, L), a * dv:(a + 1) * dv] = (hn * _sigmoid(og)).astype(o_ref.dtype)
        a_last = jnp.max(a_row, axis=1, keepdims=True)
        mm_last = jnp.maximum(a_last, m_prev)
        g_last = jnp.sum(lf_row, axis=1, keepdims=True)
        wu_col = jnp.exp(a_col - mm_last)
        decay = jnp.exp(m_prev - mm_last)
        wv = (wu_col * v.astype(F32)).astype(BF16)
        c_ref[a] = decay * c_prev + _dot_tn(k, wv)
        n_ref[a] = decay * n_ref[a] + jnp.sum(wu_col * k.astype(F32), axis=0, keepdims=True)
        m_ref[a] = g_last + mm_last

    def body(j, carry):
        r0 = pl.multiple_of(j * L, L)
        for a in range(nh):
            head_step(a, j, r0)
        return carry

    lax.fori_loop(0, n_chunks, body, 0)


def _mlstm_cell(p, gates_c, norm_g, batch, seq, heads, dk, dv):
    L = ML_CHUNK
    lb = min(ML_BLOCK, seq)
    nb = seq // lb
    cpb = lb // L
    nh = ML_HEADS_PER_STEP
    assert heads % nh == 0 and (2 * heads * dk) % (nh * dv) == 0
    hp = heads // nh
    v0 = (2 * heads * dk) // (nh * dv)
    kern = functools.partial(_mlstm_kernel, chunk=L, scale=float(dk) ** -0.5, dk=dk, dv=dv)
    return pl.pallas_call(
        kern,
        grid=(batch, hp, nb),
        in_specs=[
            pl.BlockSpec((None, lb, nh * dk), lambda b, h, c: (b, c, h)),
            pl.BlockSpec((None, lb, nh * dk), lambda b, h, c: (b, c, hp + h)),
            pl.BlockSpec((None, lb, nh * dv), lambda b, h, c: (b, c, v0 + h)),
            pl.BlockSpec((None, lb, nh * dv), lambda b, h, c: (b, c, v0 + hp + h)),
            pl.BlockSpec((nh, cpb, L), lambda b, h, c: (h, b * nb + c, 0)),
            pl.BlockSpec((nh, cpb, L), lambda b, h, c: (hp + h, b * nb + c, 0)),
            pl.BlockSpec((nh, 1, dv), lambda b, h, c: (h, 0, 0)),
        ],
        out_specs=pl.BlockSpec((None, lb, nh * dv), lambda b, h, c: (b, c, h)),
        out_shape=jax.ShapeDtypeStruct((batch, seq, heads * dv), BF16),
        scratch_shapes=[pltpu.VMEM((nh, dk, dv), F32), pltpu.VMEM((nh, 1, dk), F32),
                        pltpu.VMEM((nh, 1, 1), F32)],
        compiler_params=_cparams(("parallel", "parallel", "arbitrary")),
        name="mlstm_cell",
    )(p, p, p, p, gates_c, gates_c, norm_g.reshape(heads, 1, dv))


def _mlstm_mixer(hb, w_in, b_gates, norm_g, w_out, batch, seq):
    heads = b_gates.shape[0] // 2
    dv = norm_g.shape[1]
    dk = (w_in.shape[1] - 2 * heads - 2 * heads * dv) // (2 * heads)
    n_main = 2 * heads * dk + 2 * heads * dv
    t = batch * seq
    w_t = jnp.swapaxes(w_in, 0, 1)
    p = _matmul(hb, w_t, BF16, n=n_main, transposed=True)
    w_g_t = w_t[n_main:].astype(BF16)
    gates = _gates_t(hb, w_g_t, b_gates.reshape(-1, 1), ls_from=heads)
    gates_c = gates.reshape(2 * heads, t // ML_CHUNK, ML_CHUNK)
    cell = _mlstm_cell(p.reshape(batch, seq, n_main), gates_c, norm_g, batch, seq, heads, dk, dv)
    return _matmul(cell.reshape(t, heads * dv), w_out, F32)


def _rglru_kernel(y_ref, u_ref, cw_ref, cb_ref, wa_ref, ba_ref, wx_ref, bx_ref, lam_ref, o_ref,
                  tail_ref, hc_ref):
    c = pl.program_id(2)

    @pl.when(c == 0)
    def _():
        tail_ref[...] = jnp.zeros_like(tail_ref)
        hc_ref[...] = jnp.zeros_like(hc_ref)

    ts = u_ref.shape[0]
    u = u_ref[...].astype(F32)
    cw = cw_ref[...]
    taps = cw.shape[0]
    ext = jnp.concatenate([tail_ref[...], u], axis=0)
    uc = cw[taps - 1:taps, :] * u + cb_ref[...]
    for j in range(taps - 1):
        d = taps - 1 - j
        uc = uc + cw[j:j + 1, :] * ext[8 - d:8 - d + ts, :]
    tail_ref[...] = u[ts - 8:, :]

    ub = uc.astype(BF16)
    r = _sigmoid(_dot(ub, wa_ref[...]) + ba_ref[...])
    i = _sigmoid(_dot(ub, wx_ref[...]) + bx_ref[...])
    log_a = -RG_C * r * _softplus(-lam_ref[...])
    a = jnp.exp(log_a)
    mult = jnp.sqrt(-_expm1(2.0 * log_a))
    first = jnp.logical_and(c == 0, _iota(a.shape, 0) == 0)
    mult = jnp.where(first, 1.0, mult)
    bv = mult * i * uc
    d = 1
    while d < ts:
        a_sh = _shift_rows(a, d, 1.0)
        b_sh = _shift_rows(bv, d, 0.0)
        bv = a * b_sh + bv
        a = a * a_sh
        d *= 2
    h = bv + a * hc_ref[...]
    hc_ref[...] = h[ts - 1:ts, :]
    o_ref[...] = (h * _gelu_tanh(y_ref[...].astype(F32))).astype(o_ref.dtype)


def _rglru_cell(p, conv_w, conv_b, w_ga, b_ga, w_gx, b_gx, lam, batch, seq):
    heads, blk = b_ga.shape
    width = heads * blk
    ts = min(RG_TILE, seq)
    taps = conv_w.shape[0]
    hmap = lambda b, h, c: (h, 0, 0)
    cmap = lambda b, h, c: (0, h)
    return pl.pallas_call(
        _rglru_kernel,
        grid=(batch, heads, seq // ts),
        in_specs=[
            pl.BlockSpec((None, ts, blk), lambda b, h, c: (b, c, h)),
            pl.BlockSpec((None, ts, blk), lambda b, h, c: (b, c, heads + h)),
            pl.BlockSpec((taps, blk), cmap),
            pl.BlockSpec((1, blk), cmap),
            pl.BlockSpec((None, blk, blk), hmap),
            pl.BlockSpec((None, 1, blk), hmap),
            pl.BlockSpec((None, blk, blk), hmap),
            pl.BlockSpec((None, 1, blk), hmap),
            pl.BlockSpec((1, blk), cmap),
        ],
        out_specs=pl.BlockSpec((None, ts, blk), lambda b, h, c: (b, c, h)),
        out_shape=jax.ShapeDtypeStruct((batch, seq, width), BF16),
        scratch_shapes=[pltpu.VMEM((8, blk), F32), pltpu.VMEM((1, blk), F32)],
        compiler_params=_cparams(("parallel", "parallel", "arbitrary")),
        name="rglru_cell",
    )(p, p, conv_w, conv_b.reshape(1, width), w_ga.astype(BF16), b_ga.reshape(heads, 1, blk),
      w_gx.astype(BF16), b_gx.reshape(heads, 1, blk), lam.reshape(1, width))


def _rglru_mixer(hb, w_in, conv_w, conv_b, w_ga, b_ga, w_gx, b_gx, lam, w_out, batch, seq):
    t = batch * seq
    width = conv_b.shape[0]
    p = _matmul(hb, w_in, BF16)
    cell = _rglru_cell(p.reshape(batch, seq, 2 * width), conv_w, conv_b, w_ga, b_ga, w_gx, b_gx,
                       lam, batch, seq)
    return _matmul(cell.reshape(t, width), w_out, F32)


def _cumsum_lanes_kernel(x_ref, o_ref):
    x = x_ref[...]
    n = x.shape[1]
    lane = _iota(x.shape, 1) % 128
    d = 1
    while d < 128:
        x = x + jnp.where(lane >= d, pltpu.roll(x, d, 1), 0.0)
        d *= 2
    carry = jnp.zeros((x.shape[0], 1), F32)
    for b in range(n // 128):
        blk = x[:, b * 128:(b + 1) * 128] + carry
        o_ref[:, b * 128:(b + 1) * 128] = blk
        carry = blk[:, 127:128]


def _cumsum_lanes(x, seg):
    g, t = x.shape
    return pl.pallas_call(
        _cumsum_lanes_kernel,
        grid=(t // seg,),
        in_specs=[pl.BlockSpec((g, seg), lambda i: (0, i))],
        out_specs=pl.BlockSpec((g, seg), lambda i: (0, i)),
        out_shape=jax.ShapeDtypeStruct((g, t), F32),
        compiler_params=_cparams(("parallel",)),
        name="cumsum_lanes",
    )(x)


def _fox_kernel(q_ref, k_ref, v_ref, f_ref, o_ref, v1_ref, *, scale, hd, tk):
    qi = pl.program_id(2)
    tq = q_ref.shape[0]
    n_diag = tq // tk
    nh = q_ref.shape[1] // hd

    @pl.when(qi == 0)
    def _():
        for a in range(nh):
            v1_ref[a, :, :hd] = v_ref[:, a * hd:(a + 1) * hd]
            v1_ref[a, :, hd:] = jnp.ones((v_ref.shape[0], hd), v1_ref.dtype)

    qs = [(q_ref[:, a * hd:(a + 1) * hd].astype(F32) * scale).astype(BF16) for a in range(nh)]

    def step(j, carry, diag_off):
        c0 = pl.multiple_of(j * tk, tk)
        out = []
        for a in range(nh):
            m, acc = carry[a]
            kb = k_ref[pl.ds(c0, tk), a * hd:(a + 1) * hd]
            vb = v1_ref[a, pl.ds(c0, tk), :]
            fk = f_ref[a, :, pl.ds(c0, tk)]
            s = _dot_nt(qs[a], kb) - fk
            if diag_off is not None:
                s = jnp.where(_iota((tq, tk), 0) >= _iota((tq, tk), 1) + diag_off, s, NEG_BIG)
            m_new = jnp.maximum(m, jnp.max(s, axis=1, keepdims=True))
            p = jnp.exp((s - m_new).astype(BF16))
            out.append((m_new, jnp.exp(m - m_new) * acc + _dot(p, vb)))
        return tuple(out)

    init = tuple((jnp.full((tq, 1), NEG_BIG, F32), jnp.zeros((tq, 2 * hd), F32)) for _ in range(nh))
    final = lax.fori_loop(0, qi * n_diag, functools.partial(step, diag_off=None), init)
    for r in range(n_diag):
        final = step(qi * n_diag + r, final, r * tk)
    for a in range(nh):
        acc = final[a][1]
        o_ref[:, a * hd:(a + 1) * hd] = (acc[:, :hd] / acc[:, hd:hd + 1]).astype(o_ref.dtype)


def _fox_attention(p, f_cum, batch, seq, heads, hd):
    tq = min(FX_TQ, seq)
    tk = min(FX_TK, tq)
    assert tq % tk == 0
    nh = FX_HEADS_PER_STEP
    assert heads % nh == 0
    hp = heads // nh
    w = nh * hd
    return pl.pallas_call(
        functools.partial(_fox_kernel, scale=float(hd) ** -0.5, hd=hd, tk=tk),
        grid=(batch, hp, seq // tq),
        in_specs=[
            pl.BlockSpec((None, tq, w), lambda b, h, i: (b, i, h)),
            pl.BlockSpec((None, seq, w), lambda b, h, i: (b, 0, hp + h)),
            pl.BlockSpec((None, seq, w), lambda b, h, i: (b, 0, 2 * hp + h)),
            pl.BlockSpec((nh, 1, seq), lambda b, h, i: (h, 0, b)),
        ],
        out_specs=pl.BlockSpec((None, tq, w), lambda b, h, i: (b, i, h)),
        out_shape=jax.ShapeDtypeStruct((batch, seq, heads * hd), BF16),
        scratch_shapes=[pltpu.VMEM((nh, seq, 2 * hd), BF16)],
        compiler_params=_cparams(("parallel", "parallel", "arbitrary"), VMEM_LIMIT_BIG),
        name="fox_attention",
    )(p, p, p, f_cum)


def _fox_mixer(hb, w_in, b_f, w_out, batch, seq):
    heads = b_f.shape[0]
    d_model = w_out.shape[0]
    hd = d_model // heads
    t = batch * seq
    w_t = jnp.swapaxes(w_in, 0, 1)
    p = _matmul(hb, w_t, BF16, n=3 * d_model, transposed=True)
    w_f_t = w_t[3 * d_model:].astype(BF16)
    log_f = _gates_t(hb, w_f_t, b_f.reshape(-1, 1), ls_from=0)
    f_cum = _cumsum_lanes(log_f, seq).reshape(heads, 1, t)
    o = _fox_attention(p.reshape(batch, seq, 3 * d_model), f_cum, batch, seq, heads, hd)
    return _matmul(o.reshape(t, d_model), w_out, F32)


def _lower_bounds_kernel(g_ref, o_ref):
    g = g_ref[...]
    e = jnp.exp(g - jnp.max(g, axis=0, keepdims=True))
    p = e / jnp.sum(e, axis=0, keepdims=True)
    layer = _iota(p.shape, 0)
    rows = [jnp.sum(jnp.where(layer <= i, p, 0.0), axis=0, keepdims=True) for i in range(g.shape[0])]
    o_ref[...] = jnp.concatenate(rows, axis=0) - p[0:1, :]


def _lower_bounds(gamma):
    return pl.pallas_call(
        _lower_bounds_kernel,
        out_shape=jax.ShapeDtypeStruct(gamma.shape, F32),
        name="hgrn_lower_bounds",
    )(gamma)


def _hgrn_kernel(q_ref, f_ref, i_ref, g_ref, lb_ref, ng_ref, o_ref, s_ref):
    @pl.when(pl.program_id(2) == 0)
    def _():
        s_ref[...] = jnp.zeros_like(s_ref)

    lc, d = q_ref.shape
    q = _silu(q_ref[...].astype(F32))
    lb = lb_ref[...]
    la = jnp.log(lb)
    b_ = jnp.log1p(-lb) + _log_sigmoid(f_ref[...].astype(F32))
    log_f = jnp.maximum(la, b_) + jnp.log1p(jnp.exp(-jnp.abs(la - b_)))
    k = -_expm1(log_f)
    vb = i_ref[...]
    v = vb.astype(F32)
    c = log_f
    sh = 1
    while sh < lc:
        c = c + _shift_rows(c, sh, 0.0)
        sh *= 2
    c = c * LOG2_E
    tot = c[lc - 1:lc, :]

    s_prev = s_ref[...]
    o = _dot((q * jnp.exp2(c)).astype(BF16), s_prev.astype(BF16))
    row = _iota((lc, lc), 0)
    col = _iota((lc, lc), 1)
    x = jnp.where(row > col, row ^ col, 0)
    amat = jnp.zeros((lc, lc), F32)
    b = HG_BASE
    while b < lc:
        rho = jnp.broadcast_to(c.reshape(lc // (2 * b), 2 * b, d)[:, b - 1:b, :],
                               (lc // (2 * b), 2 * b, d)).reshape(lc, d)
        qq = (q * jnp.exp2(c - rho)).astype(BF16)
        kk = (k * jnp.exp2(rho - c)).astype(BF16)
        amat = jnp.where(jnp.logical_and(x >= b, x < 2 * b), _dot_nt(qq, kk), amat)
        b *= 2
    o = o + _dot(amat.astype(BF16), vb)
    nb = lc // HG_BASE
    q3 = q.reshape(nb, HG_BASE, d)
    k3 = k.reshape(nb, HG_BASE, d)
    c3 = c.reshape(nb, HG_BASE, d)
    v3 = v.reshape(nb, HG_BASE, d)
    tpos = _iota((nb, HG_BASE, d), 1)
    od = jnp.zeros((nb, HG_BASE, d), F32)
    for s in range(HG_BASE):
        term = q3 * k3[:, s:s + 1, :] * jnp.exp2(c3 - c3[:, s:s + 1, :])
        a_s = jnp.sum(jnp.where(tpos >= s, term, 0.0), axis=2, keepdims=True)
        od = od + a_s * v3[:, s:s + 1, :]
    o = o + od.reshape(lc, d)
    kd = (k * jnp.exp2(tot - c)).astype(BF16)
    s_ref[...] = _row_to_col(jnp.exp2(tot)) * s_prev + _dot_tn(kd, vb)
    on = o * lax.rsqrt(jnp.mean(o * o, axis=1, keepdims=True) + RMS_EPS) * ng_ref[...]
    o_ref[...] = (on * _silu(g_ref[...].astype(F32))).astype(o_ref.dtype)


def _hgrn_cell(p, lb, norm_g, batch, seq):
    heads, d = norm_g.shape
    lc = min(HG_CHUNK, seq)
    return pl.pallas_call(
        _hgrn_kernel,
        grid=(batch, heads, seq // lc),
        in_specs=[
            pl.BlockSpec((None, lc, d), lambda b, h, c: (b, c, h)),
            pl.BlockSpec((None, lc, d), lambda b, h, c: (b, c, heads + h)),
            pl.BlockSpec((None, lc, d), lambda b, h, c: (b, c, 2 * heads + h)),
            pl.BlockSpec((None, lc, d), lambda b, h, c: (b, c, 3 * heads + h)),
            pl.BlockSpec((1, d), lambda b, h, c: (0, h)),
            pl.BlockSpec((None, 1, d), lambda b, h, c: (h, 0, 0)),
        ],
        out_specs=pl.BlockSpec((None, lc, d), lambda b, h, c: (b, c, h)),
        out_shape=jax.ShapeDtypeStruct((batch, seq, heads * d), BF16),
        scratch_shapes=[pltpu.VMEM((d, d), F32)],
        compiler_params=_cparams(("parallel", "parallel", "arbitrary")),
        name="hgrn_cell",
    )(p, p, p, p, lb.reshape(1, heads * d), norm_g.reshape(heads, 1, d))


def _hgrn_mixer(hb, w_in, lb, norm_g, w_out, batch, seq):
    t = batch * seq
    d_model = w_out.shape[0]
    p = _matmul(hb, w_in, BF16)
    cell = _hgrn_cell(p.reshape(batch, seq, 4 * d_model), lb, norm_g, batch, seq)
    return _matmul(cell.reshape(t, d_model), w_out, F32)


def _router_kernel(w_ref, x_ref, b_ref, idx_ref, wt_ref):
    logits = _dot_nt(w_ref[...], x_ref[...])
    n_e, bt = logits.shape
    gs = n_e // N_GROUPS
    scores = _sigmoid(logits)
    sel = scores + b_ref[...]
    sel3 = sel.reshape(N_GROUPS, gs, bt)
    pos3 = _iota(sel3.shape, 1)
    m1 = jnp.max(sel3, axis=1, keepdims=True)
    i1 = jnp.min(jnp.where(sel3 == m1, pos3, gs), axis=1, keepdims=True)
    m2 = jnp.max(jnp.where(pos3 == i1, -jnp.inf, sel3), axis=1, keepdims=True)
    gscore = (m1 + m2).reshape(N_GROUPS, bt)
    gid = _iota(gscore.shape, 0)
    gmask = jnp.zeros(gscore.shape, F32)
    for _ in range(TOPK_GROUPS):
        gm = jnp.max(gscore, axis=0, keepdims=True)
        gi = jnp.min(jnp.where(gscore == gm, gid, N_GROUPS), axis=0, keepdims=True)
        hit = gid == gi
        gmask = jnp.where(hit, 1.0, gmask)
        gscore = jnp.where(hit, -jnp.inf, gscore)
    emask = jnp.broadcast_to(gmask.reshape(N_GROUPS, 1, bt), (N_GROUPS, gs, bt)).reshape(n_e, bt)
    cand = jnp.where(emask > 0.5, sel, -jnp.inf)
    eid = _iota(cand.shape, 0)
    idx_rows, w_rows = [], []
    for _ in range(TOP_K):
        cm = jnp.max(cand, axis=0, keepdims=True)
        ci = jnp.min(jnp.where(cand == cm, eid, n_e), axis=0, keepdims=True)
        hit = eid == ci
        idx_rows.append(ci)
        w_rows.append(jnp.sum(jnp.where(hit, scores, 0.0), axis=0, keepdims=True))
        cand = jnp.where(hit, -jnp.inf, cand)
    w = jnp.concatenate(w_rows, axis=0)
    w = w / jnp.sum(w, axis=0, keepdims=True) * ROUTED_SCALE
    idx_ref[...] = jnp.concatenate(idx_rows, axis=0)
    eye = (_iota((bt, bt), 0) == _iota((bt, bt), 1)).astype(BF16)
    w1 = w.astype(BF16)
    r1 = w - w1.astype(F32)
    w2 = r1.astype(BF16)
    w3 = (r1 - w2.astype(F32)).astype(BF16)
    wt_ref[...] = _dot_nt(eye, w1) + _dot_nt(eye, w2) + _dot_nt(eye, w3)


def _router(hb, w_router_t, bias_col, bt=512):
    t, d = hb.shape
    n_e = w_router_t.shape[0]
    bt = min(bt, t)
    return pl.pallas_call(
        _router_kernel,
        grid=(t // bt,),
        in_specs=[pl.BlockSpec((n_e, d), lambda i: (0, 0)),
                  pl.BlockSpec((bt, d), lambda i: (i, 0)),
                  pl.BlockSpec((n_e, 1), lambda i: (0, 0))],
        out_specs=[pl.BlockSpec((TOP_K, bt), lambda i: (0, i)),
                   pl.BlockSpec((bt, TOP_K), lambda i: (i, 0))],
        out_shape=[jax.ShapeDtypeStruct((TOP_K, t), I32), jax.ShapeDtypeStruct((t, TOP_K), F32)],
        compiler_params=_cparams(("parallel",)),
        name="moe_router",
    )(w_router_t, hb, bias_col)


def _plan_kernel(idx_ref, pos_ref, te_ref, tv_ref, tb_ref, *, n_e, tm, bt):
    k_top, t = idx_ref.shape
    n_steps = t // bt
    eid = _iota((n_e, bt), 0)
    upper = (_iota((bt, bt), 0) <= _iota((bt, bt), 1)).astype(BF16)

    def onehots(j):
        idx = idx_ref[:, pl.ds(pl.multiple_of(j * bt, bt), bt)]
        return [eid == idx[k:k + 1, :] for k in range(k_top)]

    def rank_body(j, carry):
        hits = onehots(j)
        m = jnp.zeros((n_e, bt), F32)
        for hmask in hits:
            m = m + hmask.astype(F32)
        cum = _dot(m.astype(BF16), upper)
        rank = carry + cum - m
        rows = [jnp.sum(jnp.where(hmask, rank, 0.0), axis=0, keepdims=True) for hmask in hits]
        pos_ref[:, pl.ds(pl.multiple_of(j * bt, bt), bt)] = jnp.concatenate(rows, axis=0).astype(I32)
        return carry + cum[:, bt - 1:bt]

    counts = lax.fori_loop(0, n_steps, rank_body, jnp.zeros((n_e, 1), F32))
    shift = tm.bit_length() - 1
    padded = (((counts.astype(I32) + (tm - 1)) >> shift) << shift).astype(F32)
    padded_row = _col_to_row(padded)
    below = _iota((n_e, n_e), 1) < _iota((n_e, n_e), 0)
    off = jnp.sum(jnp.where(below, padded_row, 0.0), axis=1, keepdims=True)

    def off_body(j, carry):
        hits = onehots(j)
        rows = [jnp.sum(jnp.where(hmask, off, 0.0), axis=0, keepdims=True) for hmask in hits]
        sl = pl.ds(pl.multiple_of(j * bt, bt), bt)
        pos_ref[:, sl] = pos_ref[:, sl] + jnp.concatenate(rows, axis=0).astype(I32)
        return carry

    lax.fori_loop(0, n_steps, off_body, 0)
    n_tiles = te_ref.shape[1]
    start = (_iota((n_e, n_tiles), 1) * tm).astype(F32)
    inside = jnp.logical_and(start >= off, start < off + padded)
    ecol = _iota((n_e, n_tiles), 0).astype(F32)
    te = jnp.sum(jnp.where(inside, ecol, 0.0), axis=0, keepdims=True)
    valid = jnp.sum(inside.astype(F32), axis=0, keepdims=True)
    last_e = jnp.max(jnp.where(padded > 0.0, ecol[:, 0:1], 0.0), axis=0, keepdims=True)
    te_ref[...] = jnp.where(valid > 0.0, te, last_e).astype(I32)
    tv_ref[...] = valid.astype(I32)
    n_valid = jnp.sum(valid, axis=1, keepdims=True)
    tb_ref[...] = jnp.minimum(_iota((1, n_tiles), 1).astype(F32), n_valid - 1.0).astype(I32)


def _plan(idx, n_e, tm, n_tiles):
    k_top, t = idx.shape
    bt = min(512, t)
    return pl.pallas_call(
        functools.partial(_plan_kernel, n_e=n_e, tm=tm, bt=bt),
        out_shape=[jax.ShapeDtypeStruct((k_top, t), I32),
                   jax.ShapeDtypeStruct((1, n_tiles), I32),
                   jax.ShapeDtypeStruct((1, n_tiles), I32),
                   jax.ShapeDtypeStruct((1, n_tiles), I32)],
        compiler_params=pltpu.CompilerParams(vmem_limit_bytes=VMEM_LIMIT_MID),
        name="moe_plan",
    )(idx)


def _dispatch_kernel(pos_ref, x_ref, xs_hbm, sem):
    k_top, bt = pos_ref.shape

    def start_body(t, carry):
        for k in range(k_top):
            pltpu.make_async_copy(x_ref.at[pl.ds(t, 1)], xs_hbm.at[pl.ds(pos_ref[k, t], 1)],
                                  sem).start(priority=k % 2)
        return carry

    lax.fori_loop(0, bt, start_body, 0, unroll=DMA_UNROLL)
    for k in range(k_top):
        pltpu.make_async_copy(x_ref, xs_hbm.at[pl.ds(0, bt)], sem).wait()


def _dispatch(pos, x_packed, n_rows):
    k_top, t = pos.shape
    half = x_packed.shape[1]
    bt = min(DISPATCH_TT, t)
    return pl.pallas_call(
        _dispatch_kernel,
        grid=(t // bt,),
        in_specs=[pl.BlockSpec((k_top, bt), lambda i: (0, i), memory_space=pltpu.SMEM),
                  pl.BlockSpec((bt, half), lambda i: (i, 0))],
        out_specs=pl.BlockSpec(memory_space=pl.ANY),
        out_shape=jax.ShapeDtypeStruct((n_rows, half), U32),
        scratch_shapes=[pltpu.SemaphoreType.DMA(())],
        compiler_params=_cparams(("arbitrary",)),
        name="moe_dispatch",
    )(pos, x_packed)


def _experts_kernel(te_ref, tv_ref, tb_ref, x_ref, wg_ref, wu_ref, wd_ref, o_ref,
                    wgb_ref, wub_ref, wdb_ref):
    i = pl.program_id(0)

    @pl.when(jnp.logical_or(i == 0, te_ref[i] != te_ref[jnp.maximum(i - 1, 0)]))
    def _():
        wgb_ref[...] = wg_ref[...].astype(BF16)
        wub_ref[...] = wu_ref[...].astype(BF16)
        wdb_ref[...] = wd_ref[...].astype(BF16)

    @pl.when(tv_ref[i] > 0)
    def _():
        half = x_ref.shape[1]
        lo, hi = _unpack_bf16_pair(x_ref[...])
        xl = lo.astype(BF16)
        xh = hi.astype(BF16)
        g = _dot_nt(xl, wgb_ref[:, :half]) + _dot_nt(xh, wgb_ref[:, half:])
        u = _dot_nt(xl, wub_ref[:, :half]) + _dot_nt(xh, wub_ref[:, half:])
        hmid = (_silu(g) * u).astype(BF16)
        y = _dot(hmid, wdb_ref[...])
        o_ref[...] = _pack_bf16_pair(y[:, :half], y[:, half:])


def _experts(xs, tile_expert, tile_valid, tile_block, w_gate, w_up, w_down, layer, tm):
    n_rows, half = xs.shape
    _, n_e, ff, d = w_gate.shape
    n_tiles = n_rows // tm
    wmap = lambda i, te, tv, tb: (layer, te[i], 0, 0)
    xmap = lambda i, te, tv, tb: (tb[i], 0)
    wspec = pl.BlockSpec((None, None, ff, d), wmap)
    return pl.pallas_call(
        _experts_kernel,
        grid_spec=pltpu.PrefetchScalarGridSpec(
            num_scalar_prefetch=3,
            grid=(n_tiles,),
            in_specs=[pl.BlockSpec((tm, half), xmap), wspec, wspec, wspec],
            out_specs=pl.BlockSpec((tm, half), xmap),
            scratch_shapes=[pltpu.VMEM((ff, d), BF16), pltpu.VMEM((ff, d), BF16),
                            pltpu.VMEM((ff, d), BF16)],
        ),
        out_shape=jax.ShapeDtypeStruct((n_rows, half), U32),
        compiler_params=_cparams(("arbitrary",), VMEM_LIMIT_BIG),
        name="moe_experts",
    )(tile_expert, tile_valid, tile_block, xs, w_gate, w_up, w_down)


def _combine_kernel(pos_ref, wt_ref, h_ref, wsg_ref, wsu_ref, wsd_ref, g_ref, b_ref, y_hbm,
                    of_ref, ob_ref, ybuf, sem, wsgb_ref, wsub_ref, wsdb_ref, *, alpha):
    k_top, bt = pos_ref.shape

    def start_body(t, carry):
        for k in range(k_top):
            pltpu.make_async_copy(y_hbm.at[pl.ds(pos_ref[k, t], 1)], ybuf.at[k, pl.ds(t, 1)], sem).start()
        return carry

    lax.fori_loop(0, bt, start_body, 0, unroll=DMA_UNROLL)

    @pl.when(pl.program_id(0) == 0)
    def _():
        wsgb_ref[...] = wsg_ref[...].astype(BF16)
        wsub_ref[...] = wsu_ref[...].astype(BF16)
        wsdb_ref[...] = wsd_ref[...].astype(BF16)

    h = h_ref[...]
    xb = h.astype(BF16)
    mid = (_silu(_dot_nt(xb, wsgb_ref[...])) * _dot_nt(xb, wsub_ref[...])).astype(BF16)
    of_ref[...] = alpha * h + _dot(mid, wsdb_ref[...])
    half = h.shape[1] // 2

    for k in range(k_top):
        pltpu.make_async_copy(y_hbm.at[pl.ds(0, bt)], ybuf.at[k], sem).wait()

    wt = wt_ref[...]
    ch = min(COMBINE_CHUNK, half)
    wb = [jnp.broadcast_to(wt[:, k:k + 1], (bt, ch)) for k in range(k_top)]
    for c in range(half // ch):
        lo_sl = slice(c * ch, (c + 1) * ch)
        hi_sl = slice(half + c * ch, half + (c + 1) * ch)
        acc_lo = of_ref[:, lo_sl]
        acc_hi = of_ref[:, hi_sl]
        for k in range(k_top):
            lo, hi = _unpack_bf16_pair(ybuf[k, :, lo_sl])
            acc_lo = acc_lo + wb[k] * lo
            acc_hi = acc_hi + wb[k] * hi
        of_ref[:, lo_sl] = acc_lo
        of_ref[:, hi_sl] = acc_hi
    y = _ln_math(of_ref[...], g_ref[...], b_ref[...])
    of_ref[...] = y
    ob_ref[...] = y.astype(BF16)


def _combine(pos, wt, h, ws_gate, ws_up, ws_down, g, b, y, layer, alpha):
    k_top, t = pos.shape
    d = h.shape[1]
    ff = ws_gate.shape[1]
    bt = min(COMBINE_TT, t)
    row = lambda i: (i, 0)
    const = lambda i: (0, 0)
    wmap = lambda i: (layer, 0, 0)
    return pl.pallas_call(
        functools.partial(_combine_kernel, alpha=alpha),
        grid=(t // bt,),
        in_specs=[pl.BlockSpec((k_top, bt), lambda i: (0, i), memory_space=pltpu.SMEM),
                  pl.BlockSpec((bt, k_top), row),
                  pl.BlockSpec((bt, d), row),
                  pl.BlockSpec((None, ff, d), wmap), pl.BlockSpec((None, ff, d), wmap),
                  pl.BlockSpec((None, ff, d), wmap),
                  pl.BlockSpec((1, d), const), pl.BlockSpec((1, d), const),
                  pl.BlockSpec(memory_space=pl.ANY)],
        out_specs=[pl.BlockSpec((bt, d), row), pl.BlockSpec((bt, d), row)],
        out_shape=[jax.ShapeDtypeStruct((t, d), F32), jax.ShapeDtypeStruct((t, d), BF16)],
        scratch_shapes=[pltpu.VMEM((k_top, bt, d // 2), U32), pltpu.SemaphoreType.DMA(()),
                        pltpu.VMEM((ff, d), BF16), pltpu.VMEM((ff, d), BF16), pltpu.VMEM((ff, d), BF16)],
        compiler_params=_cparams(("arbitrary",), VMEM_LIMIT_BIG),
        name="moe_combine",
    )(pos, wt, h, ws_gate, ws_up, ws_down, g.reshape(1, d), b.reshape(1, d), y)


def _moe_layer(h, hb, hu, w_router, router_bias, w_gate, w_up, w_down, ws_gate, ws_up, ws_down,
               ln_g, ln_b, layer, alpha):
    t, d = h.shape
    n_e = w_router.shape[1]
    tm = MOE_TM
    n_rows = t * TOP_K + n_e * tm
    n_tiles = n_rows // tm
    idx, wt = _router(hb, w_router.T.astype(BF16), router_bias.reshape(n_e, 1))
    pos, tile_expert, tile_valid, tile_block = _plan(idx, n_e, tm, n_tiles)
    xs = _dispatch(pos, hu, n_rows)
    y = _experts(xs, tile_expert.reshape(n_tiles), tile_valid.reshape(n_tiles),
                 tile_block.reshape(n_tiles), jnp.swapaxes(w_gate, 2, 3), jnp.swapaxes(w_up, 2, 3), w_down, layer, tm)
    return _combine(pos, wt, h, jnp.swapaxes(ws_gate, 1, 2), jnp.swapaxes(ws_up, 1, 2), ws_down,
                    ln_g, ln_b, y, layer, alpha)


def kernel(x, ml_w_in, ml_b_gates, ml_norm_g, ml_w_out, rg_w_in, rg_conv_w, rg_conv_b, rg_w_ga, rg_b_ga, rg_w_gx, rg_b_gx, rg_lambda, rg_w_out, fx_w_in, fx_b_f, fx_w_out, hg_w_in, hg_gamma, hg_norm_g, hg_w_out, ln1_g, ln1_b, ln2_g, ln2_b, moe_w_router, moe_router_bias, moe_w_gate, moe_w_up, moe_w_down, moe_ws_gate, moe_ws_up, moe_ws_down):
    batch, seq, d = x.shape
    depth = ln1_g.shape[0]
    alpha = float((2 * depth) ** 0.25)
    t = batch * seq
    lbs = _lower_bounds(hg_gamma)
    h = x.reshape(t, d)
    hb = h.astype(BF16)
    for layer in range(depth):
        kind, j = layer % 4, layer // 4
        if kind == 0:
            mix = _mlstm_mixer(hb, ml_w_in[j], ml_b_gates[j], ml_norm_g[j], ml_w_out[j], batch, seq)
        elif kind == 1:
            mix = _rglru_mixer(hb, rg_w_in[j], rg_conv_w[j], rg_conv_b[j], rg_w_ga[j], rg_b_ga[j],
                               rg_w_gx[j], rg_b_gx[j], rg_lambda[j], rg_w_out[j], batch, seq)
        elif kind == 2:
            mix = _fox_mixer(hb, fx_w_in[j], fx_b_f[j], fx_w_out[j], batch, seq)
        else:
            mix = _hgrn_mixer(hb, hg_w_in[j], lbs[layer], hg_norm_g[j], hg_w_out[j], batch, seq)
        h, hb, hu = _residual_ln(h, mix, ln1_g[layer], ln1_b[layer], alpha)
        h, hb = _moe_layer(h, hb, hu, moe_w_router[layer], moe_router_bias[layer], moe_w_gate,
                           moe_w_up, moe_w_down, moe_ws_gate, moe_ws_up, moe_ws_down,
                           ln2_g[layer], ln2_b[layer], layer, alpha)
    return h.reshape(batch, seq, d)
```
